```python
import math
import jax, jax.numpy as jnp
from jax import lax
import numpy as np

D_MODEL = 1024
BATCH = 2
SEQ = 8192
DEPTH = 2
DEC_BATCH = 128
DEC_SEQ = 1
PAST_LEN = 16384
PAGE_SIZE = 128

SB_HEADS = 4
SB_KV_HEADS = 2
SB_HEAD_DIM = 64
SB_WIDTH = SB_HEADS * SB_HEAD_DIM
SSD_HEADS = 8
SSD_HEAD_DIM = 64
SSD_INNER = SSD_HEADS * SSD_HEAD_DIM
SSD_GROUPS = 2
SSD_STATE = 128
SSD_CONV = 4
SSD_CONV_CH = SSD_INNER + 2 * SSD_GROUPS * SSD_STATE
SSD_CHUNK = 128
MLA_HEADS = 4
MLA_NOPE = 64
MLA_ROPE = 32
MLA_V_DIM = 64
MLA_Q_RANK = 256
MLA_KV_RANK = 256
MLA_WIDTH = MLA_HEADS * MLA_V_DIM
MLA_SCALE = (MLA_NOPE + MLA_ROPE) ** -0.5
ROPE_THETA = 10000.0

MIX_WIDTH = SB_WIDTH + SSD_INNER + MLA_WIDTH
D_FF = 4 * D_MODEL
Q_BLOCK = 128
NORM_EPS = 1e-5
DEEPNORM_ALPHA = (2 * DEPTH) ** 0.25
DEEPNORM_BETA = (8 * DEPTH) ** -0.25

IN_SIZES = (SB_WIDTH, SB_KV_HEADS * SB_HEAD_DIM, SB_KV_HEADS * SB_HEAD_DIM,
            SSD_INNER, SSD_CONV_CH, SSD_HEADS,
            MLA_Q_RANK, MLA_KV_RANK, MLA_ROPE)
IN_WIDTH = sum(IN_SIZES)

kernel_name = "hybrid_sb_ssd_mla_deepnorm_step"

F32 = jnp.float32


def _split_cols(a, sizes):
    return jnp.split(a, np.cumsum(sizes)[:-1].tolist(), axis=-1)


def _rmsnorm(x, g):
    xf = x.astype(F32)
    y = xf * lax.rsqrt(jnp.mean(xf * xf, -1, keepdims=True) + NORM_EPS) * g.astype(F32)
    return y.astype(x.dtype)


def _layernorm(x, g, b):
    xf = x.astype(F32)
    mu = jnp.mean(xf, -1, keepdims=True)
    var = jnp.mean(jnp.square(xf - mu), -1, keepdims=True)
    return ((xf - mu) * lax.rsqrt(var + NORM_EPS) * g.astype(F32) + b.astype(F32)).astype(x.dtype)


def _rope(x, pos):
    half = x.shape[-1] // 2
    inv = ROPE_THETA ** (-jnp.arange(half, dtype=F32) / half)
    ang = pos.astype(F32)[:, None] * inv[None, :]
    cos, sin = jnp.cos(ang)[:, None, :], jnp.sin(ang)[:, None, :]
    xf = x.astype(F32)
    x1, x2 = xf[..., :half], xf[..., half:]
    return jnp.concatenate([x1 * cos - x2 * sin, x1 * sin + x2 * cos], -1).astype(x.dtype)


def _sweep_query_blocks(fn, q_parts, q_pos):
    T = q_pos.shape[0]
    blk = Q_BLOCK if T % Q_BLOCK == 0 else T
    nb = T // blk
    if nb == 1:
        return fn(*q_parts, q_pos)
    split = lambda a: jnp.moveaxis(a.reshape(a.shape[0], nb, blk, *a.shape[2:]), 1, 0)
    out = lax.map(lambda args: fn(*args), (*[split(a) for a in q_parts], q_pos.reshape(nb, blk)))
    out = jnp.moveaxis(out, 0, 1)
    return out.reshape(out.shape[0], T, *out.shape[3:])


def _stick_breaking_block(q, q_pos, k, v, k_pos):
    bsz, tq, _, d = q.shape
    rep = SB_HEADS // SB_KV_HEADS
    qg = q.astype(F32).reshape(bsz, tq, SB_KV_HEADS, rep, d)
    z = jnp.einsum("bqgrd,bkgd->bgrqk", qg, k.astype(F32)) * (d ** -0.5)
    mask = k_pos[None, :] < q_pos[:, None]
    log_beta = jax.nn.log_sigmoid(z)
    log_keep = jnp.where(mask, jax.nn.log_sigmoid(-z), 0.0)
    after = lax.cumsum(log_keep, axis=4, reverse=True) - log_keep
    w = jnp.where(mask, jnp.exp(log_beta + after), 0.0)
    o = jnp.einsum("bgrqk,bkgd->bqgrd", w, v.astype(F32))
    return o.reshape(bsz, tq, SB_HEADS, d)


def _mla_block(q_lat, q_pe, q_pos, ckv, kpe, k_pos):
    s = jnp.einsum("bqhc,bkc->bhqk", q_lat.astype(F32), ckv.astype(F32))
    s = s + jnp.einsum("bqhr,bkr->bhqk", q_pe.astype(F32), kpe.astype(F32))
    s = jnp.where(k_pos[None, :] <= q_pos[:, None], s * MLA_SCALE, -jnp.inf)
    p = jax.nn.softmax(s, axis=-1)
    return jnp.einsum("bhqk,bkc->bqhc", p, ckv.astype(F32))


def _ssd_chunked_scan(X, dA, bm, cm, h0):
    bsz, T, H, P = X.shape
    L = SSD_CHUNK if T % SSD_CHUNK == 0 else T
    nc = T // L
    rep = SSD_HEADS // SSD_GROUPS
    Bh = jnp.repeat(bm, rep, axis=2).reshape(bsz, nc, L, H, SSD_STATE)
    Ch = jnp.repeat(cm, rep, axis=2).reshape(bsz, nc, L, H, SSD_STATE)
    X = X.reshape(bsz, nc, L, H, P)
    cs = jnp.cumsum(dA.reshape(bsz, nc, L, H), axis=2)
    causal = jnp.tril(jnp.ones((L, L), bool))
    seg = cs[:, :, :, None, :] - cs[:, :, None, :, :]
    decay = jnp.exp(jnp.where(causal[None, None, :, :, None], seg, -jnp.inf))
    scores = jnp.einsum("bclhn,bcshn->bclsh", Ch, Bh) * decay
    y_diag = jnp.einsum("bclsh,bcshp->bclhp", scores, X)
    to_end = jnp.exp(cs[:, :, -1:, :] - cs)
    chunk_states = jnp.einsum("bclhn,bclh,bclhp->bchpn", Bh, to_end, X)
    chunk_decay = jnp.exp(cs[:, :, -1, :])

    def step(h, inp):
        s_c, d_c = inp
        return h * d_c[:, :, None, None] + s_c, h

    h_last, h_in = lax.scan(step, h0, (jnp.moveaxis(chunk_states, 1, 0), jnp.moveaxis(chunk_decay, 1, 0)))
    h_in = jnp.moveaxis(h_in, 0, 1)
    y_off = jnp.einsum("bclhn,bchpn->bclhp", Ch, h_in) * jnp.exp(cs)[..., None]
    return (y_diag + y_off).reshape(bsz, T, H, P), h_last


def _ssd_mixer(z, xbc, dt_raw, conv_buf, h0, p):
    bsz, T, _ = xbc.shape
    xpad = jnp.concatenate([conv_buf.astype(xbc.dtype), xbc], axis=1)
    conv = p["conv_b"]
    for i in range(SSD_CONV):
        conv = conv + xpad[:, i:i + T] * p["conv_w"][i]
    conv = jax.nn.silu(conv.astype(F32))
    xs, bm, cm = _split_cols(conv, (SSD_INNER, SSD_GROUPS * SSD_STATE, SSD_GROUPS * SSD_STATE))
    xs = xs.reshape(bsz, T, SSD_HEADS, SSD_HEAD_DIM)
    bm = bm.reshape(bsz, T, SSD_GROUPS, SSD_STATE)
    cm = cm.reshape(bsz, T, SSD_GROUPS, SSD_STATE)
    dt = jax.nn.softplus(dt_raw.astype(F32) + p["dt_bias"].astype(F32))
    a = -jnp.exp(p["a_log"].astype(F32))
    y, h_last = _ssd_chunked_scan(xs * dt[..., None], dt * a, bm, cm, h0.astype(F32))
    y = y + p["d_skip"].astype(F32)[:, None] * xs
    g = y.reshape(bsz, T, SSD_INNER) * jax.nn.silu(z.astype(F32))
    g = g.reshape(bsz, T, SSD_GROUPS, SSD_INNER // SSD_GROUPS)
    g = g * lax.rsqrt(jnp.mean(g * g, -1, keepdims=True) + NORM_EPS)
    out = g.reshape(bsz, T, SSD_INNER) * p["ssd_norm"].astype(F32)
    return out.astype(z.dtype), xpad[:, T:], h_last.astype(h0.dtype)


def _layer(x, pos, past, conv_buf, h0, p):
    bsz, T, _ = x.shape
    proj = jnp.einsum("btd,de->bte", x, p["w_in"])
    sq, sk, sv, z, xbc, dt_raw, cq, ckv_raw, kpe_raw = _split_cols(proj, IN_SIZES)

    q = sq.reshape(bsz, T, SB_HEADS, SB_HEAD_DIM)
    k_new = sk.reshape(bsz, T, SB_KV_HEADS, SB_HEAD_DIM)
    v_new = sv.reshape(bsz, T, SB_KV_HEADS, SB_HEAD_DIM)
    qf = jnp.einsum("btc,chd->bthd", _rmsnorm(cq, p["q_norm"]), p["w_uq"])
    q_pe = _rope(qf[..., MLA_NOPE:], pos)
    q_lat = jnp.einsum("bthd,chd->bthc", qf[..., :MLA_NOPE], p["w_uk"])
    ckv_new = _rmsnorm(ckv_raw, p["kv_norm"])
    kpe_new = _rope(kpe_raw[:, :, None, :], pos)[:, :, 0, :]

    if past is None:
        k_all, v_all, ckv_all, kpe_all, k_pos = k_new, v_new, ckv_new, kpe_new, pos
    else:
        pk, pv, pc, pe, ppos = past
        k_all = jnp.concatenate([pk.astype(k_new.dtype), k_new], axis=1)
        v_all = jnp.concatenate([pv.astype(v_new.dtype), v_new], axis=1)
        ckv_all = jnp.concatenate([pc.astype(ckv_new.dtype), ckv_new], axis=1)
        kpe_all = jnp.concatenate([pe.astype(kpe_new.dtype), kpe_new], axis=1)
        k_pos = jnp.concatenate([ppos, pos])

    sb_o = _sweep_query_blocks(lambda qb, pb: _stick_breaking_block(qb, pb, k_all, v_all, k_pos), (q,), pos)
    sb_o = _rmsnorm(sb_o.reshape(bsz, T, SB_WIDTH), p["sb_norm"]).astype(x.dtype)

    o_lat = _sweep_query_blocks(lambda ql, qp, pb: _mla_block(ql, qp, pb, ckv_all, kpe_all, k_pos), (q_lat, q_pe), pos)
    mla_o = jnp.einsum("bthc,chd->bthd", o_lat, p["w_uv"].astype(F32)).reshape(bsz, T, MLA_WIDTH)
    mla_o = _rmsnorm(mla_o, p["mla_norm"]).astype(x.dtype)

    ssd_o, conv_new, h_new = _ssd_mixer(z, xbc, dt_raw, conv_buf, h0, p)

    mixed = jnp.concatenate([sb_o, ssd_o, mla_o], axis=-1)
    y = jnp.einsum("btm,md->btd", mixed, p["w_out"])
    h = _layernorm(DEEPNORM_ALPHA * x + y, p["ln1_g"], p["ln1_b"])
    u = jax.nn.relu(jnp.einsum("btd,df->btf", h, p["w_up"]))
    f = jnp.einsum("btf,fd->btd", u * u, p["w_down"])
    out = _layernorm(DEEPNORM_ALPHA * h + f, p["ln2_g"], p["ln2_b"])
    return out, (k_new, v_new, ckv_new, kpe_new, h_new, conv_new)


def _gather_pages(cache, layer, page_table):
    rows = cache[layer, page_table]
    return rows.reshape(rows.shape[0], rows.shape[1] * rows.shape[2], *rows.shape[3:])


def setup_inputs(seed: int = 0) -> dict:
    key = jax.random.key(seed)
    ks = jax.random.split(key, 32)
    n_pages = PAST_LEN // PAGE_SIZE
    n_used = DEC_BATCH * n_pages
    n_pool = n_used + n_used // 4
    nrm = lambda k, shape, scale: scale * jax.random.normal(k, shape, F32)
    gain = lambda k, shape: 1.0 + 0.02 * jax.random.normal(k, shape, F32)
    dt0 = jnp.exp(jax.random.uniform(ks[12], (DEPTH, SSD_HEADS), F32, math.log(1e-3), math.log(1e-1)))
    return {
        "x_prompt": nrm(ks[0], (BATCH, SEQ, D_MODEL), 1.0),
        "x_sample": nrm(ks[1], (DEC_BATCH, DEC_SEQ, D_MODEL), 1.0),
        "cache_sb_k": nrm(ks[2], (DEPTH, n_pool, PAGE_SIZE, SB_KV_HEADS, SB_HEAD_DIM), 1.0),
        "cache_sb_v": nrm(ks[3], (DEPTH, n_pool, PAGE_SIZE, SB_KV_HEADS, SB_HEAD_DIM), 1.0),
        "cache_mla_ckv": nrm(ks[4], (DEPTH, n_pool, PAGE_SIZE, MLA_KV_RANK), 1.0),
        "cache_mla_kpe": nrm(ks[5], (DEPTH, n_pool, PAGE_SIZE, MLA_ROPE), 1.0),
        "state_ssm": nrm(ks[6], (DEPTH, DEC_BATCH, SSD_HEADS, SSD_HEAD_DIM, SSD_STATE), 0.1),
        "state_conv": nrm(ks[7], (DEPTH, DEC_BATCH, SSD_CONV - 1, SSD_CONV_CH), 1.0),
        "page_table": jax.random.permutation(ks[8], n_pool)[:n_used].reshape(DEC_BATCH, n_pages).astype(jnp.int32),
        "w_in": nrm(ks[9], (DEPTH, D_MODEL, IN_WIDTH), D_MODEL ** -0.5),
        "sb_norm": gain(ks[10], (DEPTH, SB_WIDTH)),
        "conv_w": nrm(ks[11], (DEPTH, SSD_CONV, SSD_CONV_CH), SSD_CONV ** -0.5),
        "conv_b": nrm(ks[13], (DEPTH, SSD_CONV_CH), 0.02),
        "dt_bias": dt0 + jnp.log(-jnp.expm1(-dt0)),
        "a_log": jnp.log(jax.random.uniform(ks[14], (DEPTH, SSD_HEADS), F32, 1.0, 16.0)),
        "d_skip": gain(ks[15], (DEPTH, SSD_HEADS)),
        "ssd_norm": gain(ks[16], (DEPTH, SSD_INNER)),
        "q_norm": gain(ks[17], (DEPTH, MLA_Q_RANK)),
        "w_uq": nrm(ks[18], (DEPTH, MLA_Q_RANK, MLA_HEADS, MLA_NOPE + MLA_ROPE), MLA_Q_RANK ** -0.5),
        "kv_norm": gain(ks[19], (DEPTH, MLA_KV_RANK)),
        "w_uk": nrm(ks[20], (DEPTH, MLA_KV_RANK, MLA_HEADS, MLA_NOPE), MLA_KV_RANK ** -0.5),
        "w_uv": nrm(ks[21], (DEPTH, MLA_KV_RANK, MLA_HEADS, MLA_V_DIM), MLA_KV_RANK ** -0.5),
        "mla_norm": gain(ks[22], (DEPTH, MLA_WIDTH)),
        "w_out": nrm(ks[23], (DEPTH, MIX_WIDTH, D_MODEL), DEEPNORM_BETA * MIX_WIDTH ** -0.5),
        "ln1_g": gain(ks[24], (DEPTH, D_MODEL)),
        "ln1_b": nrm(ks[25], (DEPTH, D_MODEL), 0.02),
        "w_up": nrm(ks[26], (DEPTH, D_MODEL, D_FF), D_MODEL ** -0.5),
        "w_down": nrm(ks[27], (DEPTH, D_FF, D_MODEL), DEEPNORM_BETA * D_FF ** -0.5),
        "ln2_g": gain(ks[28], (DEPTH, D_MODEL)),
        "ln2_b": nrm(ks[29], (DEPTH, D_MODEL), 0.02),
    }


def reference(x_prompt, x_sample, cache_sb_k, cache_sb_v, cache_mla_ckv, cache_mla_kpe, state_ssm, state_conv,
              page_table, w_in, sb_norm, conv_w, conv_b, dt_bias, a_log, d_skip, ssd_norm, q_norm, w_uq, kv_norm,
              w_uk, w_uv, mla_norm, w_out, ln1_g, ln1_b, w_up, w_down, ln2_g, ln2_b):
    bp, seq = x_prompt.shape[0], x_prompt.shape[1]
    dec_seq = x_sample.shape[1]
    past_len = page_table.shape[1] * PAGE_SIZE
    pos_p = jnp.arange(seq, dtype=jnp.int32)
    pos_s = past_len + jnp.arange(dec_seq, dtype=jnp.int32)
    past_pos = jnp.arange(past_len, dtype=jnp.int32)

    hp, hs = x_prompt, x_sample
    new_p, new_s = [], []
    for l in range(DEPTH):
        p = {
            "w_in": w_in[l], "sb_norm": sb_norm[l], "conv_w": conv_w[l], "conv_b": conv_b[l],
            "dt_bias": dt_bias[l], "a_log": a_log[l], "d_skip": d_skip[l], "ssd_norm": ssd_norm[l],
            "q_norm": q_norm[l], "w_uq": w_uq[l], "kv_norm": kv_norm[l], "w_uk": w_uk[l], "w_uv": w_uv[l],
            "mla_norm": mla_norm[l], "w_out": w_out[l], "ln1_g": ln1_g[l], "ln1_b": ln1_b[l],
            "w_up": w_up[l], "w_down": w_down[l], "ln2_g": ln2_g[l], "ln2_b": ln2_b[l],
        }
        conv0 = jnp.zeros((bp, SSD_CONV - 1, SSD_CONV_CH), x_prompt.dtype)
        ssm0 = jnp.zeros((bp, SSD_HEADS, SSD_HEAD_DIM, SSD_STATE), x_prompt.dtype)
        hp, st_p = _layer(hp, pos_p, None, conv0, ssm0, p)
        past = (_gather_pages(cache_sb_k, l, page_table), _gather_pages(cache_sb_v, l, page_table),
                _gather_pages(cache_mla_ckv, l, page_table), _gather_pages(cache_mla_kpe, l, page_table), past_pos)
        hs, st_s = _layer(hs, pos_s, past, state_conv[l], state_ssm[l], p)
        new_p.append(st_p)
        new_s.append(st_s)

    sb_k_p = jnp.stack([s[0] for s in new_p])
    sb_v_p = jnp.stack([s[1] for s in new_p])
    ckv_p = jnp.stack([s[2] for s in new_p])
    kpe_p = jnp.stack([s[3] for s in new_p])
    ssm_p = jnp.stack([s[4] for s in new_p])
    conv_p = jnp.stack([s[5] for s in new_p])
    sb_k_s = jnp.stack([s[0] for s in new_s])
    sb_v_s = jnp.stack([s[1] for s in new_s])
    ckv_s = jnp.stack([s[2] for s in new_s])
    kpe_s = jnp.stack([s[3] for s in new_s])
    ssm_s = jnp.stack([s[4] for s in new_s])
    conv_s = jnp.stack([s[5] for s in new_s])
    return (hp, hs, sb_k_p, sb_v_p, ckv_p, kpe_p, ssm_p, conv_p, sb_k_s, sb_v_s, ckv_s, kpe_s, ssm_s, conv_s)
```

```python
import functools
import math

import jax
import jax.numpy as jnp
from jax import lax
from jax.experimental import pallas as pl
from jax.experimental.pallas import tpu as pltpu

F32 = jnp.float32
BF16 = jnp.bfloat16

D_MODEL = 1024
PAGE = 128
SB_HEADS = 4
SB_KV = 2
SB_DIM = 64
SB_W = SB_HEADS * SB_DIM
SSD_HEADS = 8
SSD_P = 64
SSD_INNER = SSD_HEADS * SSD_P
SSD_GROUPS = 2
SSD_N = 128
SSD_CONV = 4
SSD_CH = SSD_INNER + 2 * SSD_GROUPS * SSD_N
MLA_HEADS = 4
MLA_NOPE = 64
MLA_ROPE = 32
MLA_V = 64
MLA_QR = 256
MLA_KVR = 256
MLA_W = MLA_HEADS * MLA_V
MLA_SCALE = (MLA_NOPE + MLA_ROPE) ** -0.5
ROPE_THETA = 10000.0
D_FF = 4 * D_MODEL
EPS = 1e-5
DEPTH = 2
ALPHA = (2 * DEPTH) ** 0.25
LANE = 128
HEAD_PAD = 128

SB_DEAD = -104.0

VMEM_LIMIT = 56 * 1024 * 1024

C_SQ, C_SK, C_SV, C_Z, C_XBC, C_CQ, C_CKV, C_DT, C_KPE, C_END = (
    0, 256, 384, 512, 1024, 2048, 2304, 2560, 2688, 2816)


def _dot(a, b):
    return jnp.dot(a, b, preferred_element_type=F32)


def _dot_nt(a, b):
    return lax.dot_general(a, b, (((1,), (1,)), ((), ())), preferred_element_type=F32)


def _dot_tn(a, b):
    return lax.dot_general(a, b, (((0,), (0,)), ((), ())), preferred_element_type=F32)


def _split_hi_lo(x):
    hi = x.astype(BF16)
    lo = (x - hi.astype(F32)).astype(BF16)
    return hi, lo


def _dot_f32ish(x, w01):
    hi, lo = _split_hi_lo(x)
    return _dot(hi, w01) + _dot(lo, w01)


def _rms(x, g):
    return x * lax.rsqrt(jnp.mean(x * x, axis=-1, keepdims=True) + EPS) * g


def _layernorm(x, g, b):
    mu = jnp.mean(x, axis=-1, keepdims=True)
    xc = x - mu
    var = jnp.mean(xc * xc, axis=-1, keepdims=True)
    return xc * lax.rsqrt(var + EPS) * g + b


def _softplus(x):
    return jnp.maximum(x, 0.0) + jnp.log1p(jnp.exp(-jnp.abs(x)))


def _silu(x):
    return x / (1.0 + jnp.exp(-x))


def _full(shape):
    nd = len(shape)
    return pl.BlockSpec(shape, lambda *_: (0,) * nd)


def _params(sem):
    return pltpu.CompilerParams(dimension_semantics=sem, vmem_limit_bytes=VMEM_LIMIT)


def _proj_body(decode, x_ref, tab_ref, w1_ref, w2_ref, wk_ref, wv_ref, qg_ref, kvg_ref, *outs):
    if decode:
        (sq_ref, sk_ref, sv_ref, z_ref, xbc_ref, dt_ref, ckv_ref, kpe_ref, qh_ref, qlat_ref) = outs
    else:
        (sq_ref, sk_ref, sv_ref, z_ref, xbc_ref, dt_ref, ckv_ref, kpe_ref,
         kb_ref, vb_ref, qh_ref, kh_ref, vm_ref) = outs
    x = x_ref[...].astype(BF16)

    def col(lo, hi):
        return _dot(x, w1_ref[:, lo:hi])

    sq_ref[...] = col(C_SQ, C_SK)
    sk = col(C_SK, C_SV)
    sv = col(C_SV, C_Z)
    sk_ref[...] = sk
    sv_ref[...] = sv
    z_ref[...] = col(C_Z, C_XBC)
    xbc_ref[...] = col(C_XBC, C_CQ)
    dt_ref[...] = col(C_DT, C_KPE)

    cosq = tab_ref[:, 0:128]
    sinq = tab_ref[:, 128:256]
    cosk = tab_ref[:, 256:384]
    sink = tab_ref[:, 384:512]

    cqn = _rms(col(C_CQ, C_CKV), qg_ref[...]).astype(BF16)
    qa = _dot(cqn, w2_ref[:, 0:512])
    qb = _dot(cqn, w2_ref[:, 512:1024])
    qh = []
    for h in range(MLA_HEADS):
        sl = slice(h * HEAD_PAD, (h + 1) * HEAD_PAD)
        qh.append((qa[:, sl] * cosq + qb[:, sl] * sinq) * MLA_SCALE)

    ckvn = _rms(col(C_CKV, C_DT), kvg_ref[...])
    ckv_ref[...] = ckvn
    kg = col(C_KPE, C_END)
    kpe = kg * cosk + pltpu.roll(kg, 96, 1) * sink
    kpe_ref[...] = kpe[:, 0:MLA_ROPE]

    if decode:
        for h in range(MLA_HEADS):
            sl = slice(h * HEAD_PAD, (h + 1) * HEAD_PAD)
            qh_ref[:, sl] = qh[h]
            qlat_ref[:, h * MLA_KVR:(h + 1) * MLA_KVR] = _dot(qh[h].astype(BF16), wk_ref[sl, :])
    else:
        kb_ref[...] = sk.astype(BF16)
        vb_ref[...] = sv.astype(BF16)
        ckvb = ckvn.astype(BF16)
        kpe_at64 = pltpu.roll(kpe, 64, 1)
        for h in range(MLA_HEADS):
            sl = slice(h * HEAD_PAD, (h + 1) * HEAD_PAD)
            qh_ref[:, sl] = qh[h].astype(BF16)
            kh_ref[:, sl] = (_dot(ckvb, wk_ref[:, sl]) + kpe_at64).astype(BF16)
        vm_ref[...] = _dot(ckvb, wv_ref[...]).astype(BF16)


def _proj_call(x, tab, w, decode, tm):
    n = x.shape[0]
    nt = tab.shape[0] // tm
    grid = (n // tm,)
    row = lambda width: pl.BlockSpec((tm, width), lambda i: (i, 0))
    in_specs = [
        row(D_MODEL),
        pl.BlockSpec((tm, 512), lambda i: (i % nt, 0)),
        _full(w["w1"].shape), _full(w["w2"].shape), _full(w["wk"].shape), _full(w["wv"].shape),
        _full((1, MLA_QR)), _full((1, MLA_KVR)),
    ]
    f = lambda width, dt=F32: jax.ShapeDtypeStruct((n, width), dt)
    out_shape = [f(256), f(128), f(128), f(512), f(1024), f(128), f(256), f(MLA_ROPE)]
    out_specs = [row(256), row(128), row(128), row(512), row(1024), row(128), row(256), row(MLA_ROPE)]
    if decode:
        out_shape += [f(512), f(MLA_HEADS * MLA_KVR)]
        out_specs += [row(512), row(MLA_HEADS * MLA_KVR)]
    else:
        out_shape += [f(128, BF16), f(128, BF16), f(512, BF16), f(512, BF16), f(256, BF16)]
        out_specs += [row(128), row(128), row(512), row(512), row(256)]
    return pl.pallas_call(
        functools.partial(_proj_body, decode),
        grid=grid, in_specs=in_specs, out_specs=out_specs, out_shape=out_shape,
        compiler_params=_params(("arbitrary",)),
        name="proj_decode" if decode else "proj_prompt",
    )(x, tab, w["w1"], w["w2"], w["wk"], w["wv"], w["q_norm"], w["kv_norm"])


def _suffix_matrix(tk):
    j = lax.broadcasted_iota(jnp.int32, (tk, tk), 0)
    s = lax.broadcasted_iota(jnp.int32, (tk, tk), 1)
    return jnp.where(j > s, 1.0, 0.0).astype(BF16)


def _sb_block(qall, kblk, vblk, umat, c, mask, keys_on_lanes=False):
    z = (_dot(qall, kblk) if keys_on_lanes else _dot_nt(qall, kblk)) * (SB_DIM ** -0.5)
    lse = jnp.log1p(jnp.exp(-jnp.abs(z)))
    log_beta = jnp.minimum(z, 0.0) - lse
    log_keep = log_beta - z
    if mask is not None:
        log_keep = jnp.where(mask, log_keep, 0.0)
    hi, lo = _split_hi_lo(log_keep)
    after = _dot(hi, umat) + _dot(lo, umat)
    w = jnp.exp(log_beta + after + c)
    if mask is not None:
        w = jnp.where(mask, w, 0.0)
    wb = w.astype(BF16)
    pv = _dot_nt(wb, vblk) if keys_on_lanes else _dot(wb, vblk)
    return pv, c + jnp.sum(log_keep, axis=1, keepdims=True)


def _sb_prompt_body(q_ref, k_ref, v_ref, o_ref, qall_ref, acc_ref, c_ref, *, tq):
    i = pl.program_id(1)
    tk = tq
    lane = lax.broadcasted_iota(jnp.int32, (tq, LANE), 1)
    lo_half = lane < SB_DIM
    s0 = q_ref[:, 0:128]
    s1 = q_ref[:, 128:256]
    qall_ref[0 * tq:1 * tq, :] = jnp.where(lo_half, s0, 0.0).astype(BF16)
    qall_ref[1 * tq:2 * tq, :] = jnp.where(lo_half, pltpu.roll(s0, 64, 1), 0.0).astype(BF16)
    qall_ref[2 * tq:3 * tq, :] = jnp.where(lo_half, 0.0, pltpu.roll(s1, 64, 1)).astype(BF16)
    qall_ref[3 * tq:4 * tq, :] = jnp.where(lo_half, 0.0, s1).astype(BF16)
    umat = _suffix_matrix(tk)

    row = lax.rem(lax.broadcasted_iota(jnp.int32, (SB_HEADS * tq, tk), 0), tq)
    colm = lax.broadcasted_iota(jnp.int32, (SB_HEADS * tq, tk), 1)
    mask = colm < row

    def run(kb, c, m):
        start = pl.multiple_of(kb * tk, tk)
        return _sb_block(qall_ref[...], k_ref[pl.ds(start, tk), :], v_ref[pl.ds(start, tk), :], umat, c, m)

    pv, c = run(i, jnp.zeros((SB_HEADS * tq, 1), F32), mask)
    acc_ref[...] = pv
    c_ref[...] = c

    def cond(st):
        kb, cmax = st
        return jnp.logical_and(kb >= 0, cmax > SB_DEAD)

    def body(st):
        kb, _ = st
        pv, c = run(kb, c_ref[...], None)
        acc_ref[...] += pv
        c_ref[...] = c
        return kb - 1, jnp.max(c)

    lax.while_loop(cond, body, (i - 1, jnp.max(c)))

    o0 = acc_ref[0 * tq:1 * tq, :]
    o1 = acc_ref[1 * tq:2 * tq, :]
    o2 = acc_ref[2 * tq:3 * tq, :]
    o3 = acc_ref[3 * tq:4 * tq, :]
    o_ref[:, 0:128] = jnp.where(lo_half, o0, pltpu.roll(o1, 64, 1))
    o_ref[:, 128:256] = jnp.where(lo_half, pltpu.roll(o2, 64, 1), o3)


def _sb_prompt_call(sq, kb, vb, bsz, seq, tq):
    nq = seq // tq
    return pl.pallas_call(
        functools.partial(_sb_prompt_body, tq=tq),
        grid=(bsz, nq),
        in_specs=[
            pl.BlockSpec((tq, SB_W), lambda b, i: (b * nq + i, 0)),
            pl.BlockSpec((seq, 128), lambda b, i: (b, 0)),
            pl.BlockSpec((seq, 128), lambda b, i: (b, 0)),
        ],
        out_specs=pl.BlockSpec((tq, SB_W), lambda b, i: (b * nq + i, 0)),
        out_shape=jax.ShapeDtypeStruct((bsz * seq, SB_W), F32),
        scratch_shapes=[
            pltpu.VMEM((SB_HEADS * tq, LANE), BF16),
            pltpu.VMEM((SB_HEADS * tq, LANE), F32),
            pltpu.VMEM((SB_HEADS * tq, 1), F32),
        ],
        compiler_params=_params(("arbitrary", "arbitrary")),
        name="sb_prompt",
    )(sq, kb, vb)


SB_DEC_PAGES = 2


def _sb_decode_body(pt_ref, q_ref, kc_ref, vc_ref, o_ref, kbuf, vbuf, sem, *, layer, n_pages):
    s = pl.program_id(0)
    g_pages = SB_DEC_PAGES
    ng = n_pages // g_pages
    tk = g_pages * PAGE

    def copies(g, slot):
        out = []
        for j in range(g_pages):
            page = pt_ref[s, g * g_pages + j]
            dst = pl.ds(j * PAGE, PAGE)
            out.append(pltpu.make_async_copy(kc_ref.at[layer, page], kbuf.at[slot, :, dst], sem.at[0, slot]))
            out.append(pltpu.make_async_copy(vc_ref.at[layer, page], vbuf.at[slot, :, dst], sem.at[1, slot]))
        return out

    def start(g, slot):
        for cp in copies(g, slot):
            cp.start()

    def wait(g, slot):
        for cp in copies(g, slot):
            cp.wait()

    slot_of = lambda g: (ng - 1 - g) % 2
    umat = _suffix_matrix(tk)
    qall = q_ref[0].astype(BF16)
    start(ng - 1, 0)

    def cond(st):
        g, cmax, _, _ = st
        return jnp.logical_and(g >= 0, cmax > SB_DEAD)

    def body(st):
        g, _, c, acc = st
        slot = slot_of(g)

        @pl.when(g >= 1)
        def _():
            start(g - 1, 1 - slot)

        wait(g, slot)
        pv, c = _sb_block(qall, kbuf[slot].astype(BF16), vbuf[slot].astype(BF16), umat, c, None,
                          keys_on_lanes=True)
        return g - 1, jnp.max(c[0:SB_HEADS]), c, acc + pv

    g_end, _, _, acc = lax.while_loop(
        cond, body, (ng - 1, jnp.float32(0.0), jnp.zeros((8, 1), F32), jnp.zeros((8, LANE), F32)))

    @pl.when(g_end >= 0)
    def _():
        wait(g_end, slot_of(g_end))

    lane = lax.broadcasted_iota(jnp.int32, (1, LANE), 1)
    lo_half = lane < SB_DIM
    o_ref[0, :, 0:128] = jnp.where(lo_half, acc[0:1], pltpu.roll(acc[1:2], 64, 1))
    o_ref[0, :, 128:256] = jnp.where(lo_half, pltpu.roll(acc[2:3], 64, 1), acc[3:4])


def _sb_decode_call(page_table, qbd, kc, vc, layer):
    n, n_pages = page_table.shape
    grid_spec = pltpu.PrefetchScalarGridSpec(
        num_scalar_prefetch=1,
        grid=(n,),
        in_specs=[
            pl.BlockSpec((1, 8, LANE), lambda s, pt: (s, 0, 0)),
            pl.BlockSpec(memory_space=pl.ANY),
            pl.BlockSpec(memory_space=pl.ANY),
        ],
        out_specs=pl.BlockSpec((1, 1, SB_W), lambda s, pt: (s, 0, 0)),
        scratch_shapes=[
            pltpu.VMEM((2, LANE, SB_DEC_PAGES * PAGE), F32),
            pltpu.VMEM((2, LANE, SB_DEC_PAGES * PAGE), F32),
            pltpu.SemaphoreType.DMA((2, 2)),
        ],
    )
    return pl.pallas_call(
        functools.partial(_sb_decode_body, layer=layer, n_pages=n_pages),
        grid_spec=grid_spec,
        out_shape=jax.ShapeDtypeStruct((n, 1, SB_W), F32),
        compiler_params=_params(("arbitrary",)),
        name="sb_decode",
    )(page_table, qbd, kc, vc)


def _mla_prompt_body(q_ref, k_ref, v_ref, o_ref, m_ref, l_ref, acc_ref, *, tq):
    i = pl.program_id(1)
    tk = tq
    m_ref[...] = jnp.full(m_ref.shape, -jnp.inf, F32)
    l_ref[...] = jnp.zeros(l_ref.shape, F32)
    acc_ref[...] = jnp.zeros(acc_ref.shape, F32)
    row = lax.broadcasted_iota(jnp.int32, (tq, tk), 0)
    colm = lax.broadcasted_iota(jnp.int32, (tq, tk), 1)
    diag_mask = colm <= row

    def block(kb, mask):
        start = pl.multiple_of(kb * tk, tk)
        for h in range(MLA_HEADS):
            sl = slice(h * HEAD_PAD, (h + 1) * HEAD_PAD)
            vsl = slice((h // 2) * LANE, (h // 2 + 1) * LANE)
            s = _dot_nt(q_ref[:, sl], k_ref[pl.ds(start, tk), sl])
            if mask is not None:
                s = jnp.where(mask, s, -jnp.inf)
            m_old = m_ref[h]
            m_new = jnp.maximum(m_old, jnp.max(s, axis=1, keepdims=True))
            p = jnp.exp(s - m_new)
            alpha = jnp.exp(m_old - m_new)
            l_ref[h] = alpha * l_ref[h] + jnp.sum(p, axis=1, keepdims=True)
            acc_ref[h] = alpha * acc_ref[h] + _dot(p.astype(BF16), v_ref[pl.ds(start, tk), vsl])
            m_ref[h] = m_new

    def body(kb, carry):
        block(kb, None)
        return carry

    lax.fori_loop(0, i, body, 0)
    block(i, diag_mask)

    lane = lax.broadcasted_iota(jnp.int32, (tq, LANE), 1)
    lo_half = lane < MLA_V
    o = [acc_ref[h] / l_ref[h] for h in range(MLA_HEADS)]
    o_ref[:, 0:128] = jnp.where(lo_half, o[0], o[1])
    o_ref[:, 128:256] = jnp.where(lo_half, o[2], o[3])


def _mla_prompt_call(qh, kh, vm, bsz, seq, tq):
    nq = seq // tq
    return pl.pallas_call(
        functools.partial(_mla_prompt_body, tq=tq),
        grid=(bsz, nq),
        in_specs=[
            pl.BlockSpec((tq, 512), lambda b, i: (b * nq + i, 0)),
            pl.BlockSpec((seq, 512), lambda b, i: (b, 0)),
            pl.BlockSpec((seq, 256), lambda b, i: (b, 0)),
        ],
        out_specs=pl.BlockSpec((tq, MLA_W), lambda b, i: (b * nq + i, 0)),
        out_shape=jax.ShapeDtypeStruct((bsz * seq, MLA_W), F32),
        scratch_shapes=[
            pltpu.VMEM((MLA_HEADS, tq, 1), F32),
            pltpu.VMEM((MLA_HEADS, tq, 1), F32),
            pltpu.VMEM((MLA_HEADS, tq, LANE), F32),
        ],
        compiler_params=_params(("arbitrary", "arbitrary")),
        name="mla_prompt",
    )(qh, kh, vm)


MLA_DEC_PAGES = 4


def _mla_decode_body(pt_ref, ql_ref, qp_ref, cn_ref, pn_ref, wuv_ref, cc_ref, pc_ref, o_ref,
                     cbuf, pbuf, sem, *, layer, n_pages):
    s = pl.program_id(0)
    g_pages = MLA_DEC_PAGES
    ng = n_pages // g_pages

    def copies(g, slot):
        out = []
        for j in range(g_pages):
            page = pt_ref[s, g * g_pages + j]
            dst = pl.ds(j * PAGE, PAGE)
            out.append(pltpu.make_async_copy(cc_ref.at[layer, page], cbuf.at[slot, dst], sem.at[0, slot]))
            out.append(pltpu.make_async_copy(pc_ref.at[layer, page], pbuf.at[slot, :, dst], sem.at[1, slot]))
        return out

    def start(g, slot):
        for cp in copies(g, slot):
            cp.start()

    def wait(g, slot):
        for cp in copies(g, slot):
            cp.wait()

    ql = ql_ref[0]
    qp = qp_ref[0]
    qlb = ql.astype(BF16)
    qpb = qp.astype(BF16)
    start(0, 0)

    def body(g, st):
        m_old, l_old, acc = st
        slot = g % 2

        @pl.when(g + 1 < ng)
        def _():
            start(g + 1, 1 - slot)

        wait(g, slot)
        cb = cbuf[slot].astype(BF16)
        sc = _dot_nt(qlb, cb) + _dot(qpb, pbuf[slot].astype(BF16))
        m_new = jnp.maximum(m_old, jnp.max(sc, axis=1, keepdims=True))
        p = jnp.exp(sc - m_new)
        alpha = jnp.exp(m_old - m_new)
        return m_new, alpha * l_old + jnp.sum(p, axis=1, keepdims=True), alpha * acc + _dot(p.astype(BF16), cb)

    m_old, l_old, acc = lax.fori_loop(
        0, ng, body,
        (jnp.full((8, 1), -jnp.inf, F32), jnp.zeros((8, 1), F32), jnp.zeros((8, MLA_KVR), F32)))

    cn = cn_ref[0]
    pn = pn_ref[0]
    sc = jnp.sum(ql * cn, axis=1, keepdims=True) + jnp.sum(qp * pn, axis=1, keepdims=True)
    m_new = jnp.maximum(m_old, sc)
    p = jnp.exp(sc - m_new)
    alpha = jnp.exp(m_old - m_new)
    l_new = alpha * l_old + p
    o_lat = (alpha * acc + p * cn) / l_new
    o_all = _dot(o_lat.astype(BF16), wuv_ref[...])
    sub = lax.broadcasted_iota(jnp.int32, (8, MLA_W), 0)
    lane = lax.broadcasted_iota(jnp.int32, (8, MLA_W), 1)
    pick = (lane // MLA_V) == sub
    o_ref[0] = jnp.sum(jnp.where(pick, o_all, 0.0), axis=0, keepdims=True)


def _mla_decode_call(page_table, qlat, qpe, ckv_new, kpe_new, wuv, cc, pc, layer):
    n, n_pages = page_table.shape
    tk = MLA_DEC_PAGES * PAGE
    grid_spec = pltpu.PrefetchScalarGridSpec(
        num_scalar_prefetch=1,
        grid=(n,),
        in_specs=[
            pl.BlockSpec((1, 8, MLA_KVR), lambda s, pt: (s, 0, 0)),
            pl.BlockSpec((1, 8, MLA_ROPE), lambda s, pt: (s, 0, 0)),
            pl.BlockSpec((1, 1, MLA_KVR), lambda s, pt: (s, 0, 0)),
            pl.BlockSpec((1, 1, MLA_ROPE), lambda s, pt: (s, 0, 0)),
            pl.BlockSpec((MLA_KVR, MLA_W), lambda s, pt: (0, 0)),
            pl.BlockSpec(memory_space=pl.ANY),
            pl.BlockSpec(memory_space=pl.ANY),
        ],
        out_specs=pl.BlockSpec((1, 1, MLA_W), lambda s, pt: (s, 0, 0)),
        scratch_shapes=[
            pltpu.VMEM((2, tk, MLA_KVR), F32),
            pltpu.VMEM((2, MLA_ROPE, tk), F32),
            pltpu.SemaphoreType.DMA((2, 2)),
        ],
    )
    return pl.pallas_call(
        functools.partial(_mla_decode_body, layer=layer, n_pages=n_pages),
        grid_spec=grid_spec,
        out_shape=jax.ShapeDtypeStruct((n, 1, MLA_W), F32),
        compiler_params=_params(("arbitrary",)),
        name="mla_decode",
    )(page_table, qlat, qpe, ckv_new, kpe_new, wuv, cc, pc)


def _expand_matrix():
    r = lax.broadcasted_iota(jnp.int32, (LANE, SSD_INNER), 0)
    c = lax.broadcasted_iota(jnp.int32, (LANE, SSD_INNER), 1)
    return jnp.where(c // SSD_P == r, 1.0, 0.0).astype(BF16)


def _gated_norm(y, z, norm_w):
    g = y * _silu(z)
    half = SSD_INNER // SSD_GROUPS
    parts = []
    for k in range(SSD_GROUPS):
        gk = g[:, k * half:(k + 1) * half]
        parts.append(gk * lax.rsqrt(jnp.mean(gk * gk, axis=-1, keepdims=True) + EPS))
    return jnp.concatenate(parts, axis=1) * norm_w


def _ssd_prompt_body(z_ref, xbc_ref, dt_ref, cw_ref, cb_ref, dtb_ref, alog_ref, dskip_ref, nw_ref,
                     y_ref, hout_ref, cout_ref, xwin, ht, *, chunk):
    c = pl.program_id(1)
    nc = pl.num_programs(1)
    L = chunk

    @pl.when(c == 0)
    def _():
        xwin[0:8, :] = jnp.zeros((8, SSD_CH), F32)
        ht[...] = jnp.zeros(ht.shape, F32)

    xwin[8:8 + L, :] = xbc_ref[...]
    conv = cb_ref[...] + xwin[5:5 + L, :] * cw_ref[0:1, :]
    for i in range(1, SSD_CONV):
        conv = conv + xwin[5 + i:5 + i + L, :] * cw_ref[i:i + 1, :]
    xwin[0:8, :] = xwin[L:L + 8, :]
    cv = _silu(conv)
    xs = cv[:, 0:SSD_INNER]
    bm = cv[:, SSD_INNER:SSD_INNER + 256]
    cm = cv[:, SSD_INNER + 256:SSD_INNER + 512]

    dt = _softplus(dt_ref[...] + dtb_ref[...])
    da = dt * (-jnp.exp(alog_ref[...]))
    r = lax.broadcasted_iota(jnp.int32, (L, L), 0)
    s = lax.broadcasted_iota(jnp.int32, (L, L), 1)
    causal = s <= r
    tri = jnp.where(causal, 1.0, 0.0).astype(BF16)
    cs = _dot_f32ish_left(tri, da)
    cs_t = cs.T
    cs_last = cs[L - 1:L, :]
    ecs = jnp.exp(cs)
    emat = _expand_matrix()
    dt_e = _dot_f32ish(dt, emat)
    ecs_e = _dot_f32ish(ecs, emat)
    te_e = _dot_f32ish(dt * jnp.exp(cs_last - cs), emat)
    last_e = _dot_f32ish(jnp.exp(cs_last), emat)

    x_dt = (xs * dt_e).astype(BF16)
    x_end = (xs * te_e).astype(BF16)
    lane = lax.broadcasted_iota(jnp.int32, (L, LANE), 1)
    lo_half = lane < SSD_P
    y_parts = []
    for g in range(SSD_GROUPS):
        gs = slice(g * SSD_N, (g + 1) * SSD_N)
        cg = cm[:, gs].astype(BF16)
        bg = bm[:, gs].astype(BF16)
        cb = _dot_nt(cg, bg)
        hs = slice(g * 256, (g + 1) * 256)
        h_in = ht[g]
        y_off = _dot(cg, h_in.astype(BF16)) * ecs_e[:, hs]
        yd = []
        for hh in range(4):
            h = g * 4 + hh
            seg = cs[:, h:h + 1] - cs_t[h:h + 1, :]
            decay = jnp.exp(jnp.where(causal, seg, -jnp.inf))
            sc = (cb * decay).astype(BF16)
            ps = slice((h // 2) * LANE, (h // 2 + 1) * LANE)
            yd.append(_dot(sc, x_dt[:, ps]))
        y_diag = jnp.concatenate([jnp.where(lo_half, yd[0], yd[1]), jnp.where(lo_half, yd[2], yd[3])], axis=1)
        y_parts.append(y_diag + y_off)
        ht[g] = h_in * last_e[:, hs] + _dot_tn(bg, x_end[:, hs])
    y = jnp.concatenate(y_parts, axis=1) + dskip_ref[...] * xs
    y_ref[...] = _gated_norm(y, z_ref[...], nw_ref[...])

    @pl.when(c == nc - 1)
    def _():
        for g in range(SSD_GROUPS):
            hout_ref[0, g * 256:(g + 1) * 256, :] = ht[g].T
        cout_ref[0] = xwin[0:8, :][5:8, :]


def _dot_f32ish_left(w01, x):
    hi, lo = _split_hi_lo(x)
    return _dot(w01, hi) + _dot(w01, lo)


def _ssd_prompt_call(z, xbc, dt, w, bsz, seq, chunk):
    nc = seq // chunk
    row = lambda width: pl.BlockSpec((chunk, width), lambda b, c: (b * nc + c, 0))
    return pl.pallas_call(
        functools.partial(_ssd_prompt_body, chunk=chunk),
        grid=(bsz, nc),
        in_specs=[row(SSD_INNER), row(SSD_CH), row(LANE),
                  _full((SSD_CONV, SSD_CH)), _full((1, SSD_CH)), _full((1, LANE)), _full((1, LANE)),
                  _full((1, SSD_INNER)), _full((1, SSD_INNER))],
        out_specs=[row(SSD_INNER),
                   pl.BlockSpec((1, SSD_INNER, SSD_N), lambda b, c: (b, 0, 0)),
                   pl.BlockSpec((1, SSD_CONV - 1, SSD_CH), lambda b, c: (b, 0, 0))],
        out_shape=[jax.ShapeDtypeStruct((bsz * seq, SSD_INNER), F32),
                   jax.ShapeDtypeStruct((bsz, SSD_INNER, SSD_N), F32),
                   jax.ShapeDtypeStruct((bsz, SSD_CONV - 1, SSD_CH), F32)],
        scratch_shapes=[pltpu.VMEM((chunk + 8, SSD_CH), F32), pltpu.VMEM((SSD_GROUPS, SSD_N, 256), F32)],
        compiler_params=_params(("arbitrary", "arbitrary")),
        name="ssd_prompt",
    )(z, xbc, dt, w["conv_w"], w["conv_b"], w["dt_bias"], w["a_log"], w["d_skip_e"], w["ssd_norm"])


def _ssd_decode_body(z_ref, xbc_ref, dt_ref, cbuf_ref, h0_ref, cw_ref, cb_ref, dtb_ref, alog_ref, dskip_ref,
                     nw_ref, y_ref, hout_ref, cout_ref, xt, dect, bsel, csel, yt, xs_keep):
    s = pl.program_id(0)
    n = pl.num_programs(0)

    @pl.when(s == 0)
    def _():
        xnew = xbc_ref[...]
        conv = cb_ref[...] + xnew * cw_ref[3:4, :]
        for i in range(SSD_CONV - 1):
            conv = conv + cbuf_ref[i] * cw_ref[i:i + 1, :]
        cout_ref[0] = cbuf_ref[1]
        cout_ref[1] = cbuf_ref[2]
        cout_ref[2] = xnew
        cv = _silu(conv)
        xs = cv[:, 0:SSD_INNER]
        xs_keep[...] = xs
        bsel[...] = cv[:, SSD_INNER:SSD_INNER + 256]
        csel[...] = cv[:, SSD_INNER + 256:SSD_INNER + 512]
        dt = _softplus(dt_ref[...] + dtb_ref[...])
        da = dt * (-jnp.exp(alog_ref[...]))
        emat = _expand_matrix()
        xt[...] = (xs * _dot_f32ish(dt, emat)).T
        dect[...] = _dot_f32ish(jnp.exp(da), emat).T
        yt[...] = jnp.zeros(yt.shape, F32)

    lane = lax.broadcasted_iota(jnp.int32, xt.shape, 1)
    here = lane == s
    x_col = jnp.sum(jnp.where(here, xt[...], 0.0), axis=1, keepdims=True)
    d_col = jnp.sum(jnp.where(here, dect[...], 0.0), axis=1, keepdims=True)
    b_row = bsel[pl.ds(s, 1), :]
    c_row = csel[pl.ds(s, 1), :]
    ys = []
    for g in range(SSD_GROUPS):
        hs = slice(g * 256, (g + 1) * 256)
        gs = slice(g * SSD_N, (g + 1) * SSD_N)
        h_new = h0_ref[0, hs, :] * d_col[hs] + x_col[hs] * b_row[:, gs]
        hout_ref[0, hs, :] = h_new
        ys.append(jnp.sum(h_new * c_row[:, gs], axis=1, keepdims=True))
    y_col = jnp.concatenate(ys, axis=0)
    yt[...] = jnp.where(here, y_col, yt[...])

    @pl.when(s == n - 1)
    def _():
        xs = xs_keep[...]
        y = yt[...].T + dskip_ref[...] * xs
        y_ref[...] = _gated_norm(y, z_ref[...], nw_ref[...])


def _ssd_decode_call(z, xbc, dt, conv_t, h0, w):
    n = z.shape[0]
    return pl.pallas_call(
        _ssd_decode_body,
        grid=(n,),
        in_specs=[_full((n, SSD_INNER)), _full((n, SSD_CH)), _full((n, LANE)),
                  _full((SSD_CONV - 1, n, SSD_CH)),
                  pl.BlockSpec((1, SSD_INNER, SSD_N), lambda s: (s, 0, 0)),
                  _full((SSD_CONV, SSD_CH)), _full((1, SSD_CH)), _full((1, LANE)), _full((1, LANE)),
                  _full((1, SSD_INNER)), _full((1, SSD_INNER))],
        out_specs=[_full((n, SSD_INNER)),
                   pl.BlockSpec((1, SSD_INNER, SSD_N), lambda s: (s, 0, 0)),
                   _full((SSD_CONV - 1, n, SSD_CH))],
        out_shape=[jax.ShapeDtypeStruct((n, SSD_INNER), F32),
                   jax.ShapeDtypeStruct((n, SSD_INNER, SSD_N), F32),
                   jax.ShapeDtypeStruct((SSD_CONV - 1, n, SSD_CH), F32)],
        scratch_shapes=[pltpu.VMEM((SSD_INNER, n), F32), pltpu.VMEM((SSD_INNER, n), F32),
                        pltpu.VMEM((n, 256), F32), pltpu.VMEM((n, 256), F32),
                        pltpu.VMEM((SSD_INNER, n), F32), pltpu.VMEM((n, SSD_INNER), F32)],
        compiler_params=_params(("arbitrary",)),
        name="ssd_decode",
    )(z, xbc, dt, conv_t, h0, w["conv_w"], w["conv_b"], w["dt_bias"], w["a_log"], w["d_skip_e"], w["ssd_norm"])


FF_CHUNK = 1024


def _out_body(x_ref, sb_ref, ssd_ref, mla_ref, sbg_ref, mlag_ref, wo_ref, g1_ref, b1_ref,
              wup_ref, wdn_ref, g2_ref, b2_ref, o_ref):
    sbn = _rms(sb_ref[...], sbg_ref[...]).astype(BF16)
    mlan = _rms(mla_ref[...], mlag_ref[...]).astype(BF16)
    y = (_dot(sbn, wo_ref[0:SB_W, :])
         + _dot(ssd_ref[...].astype(BF16), wo_ref[SB_W:SB_W + SSD_INNER, :])
         + _dot(mlan, wo_ref[SB_W + SSD_INNER:, :]))
    h = _layernorm(ALPHA * x_ref[...] + y, g1_ref[...], b1_ref[...])
    hb = h.astype(BF16)
    f = jnp.zeros(h.shape, F32)
    for j in range(D_FF // FF_CHUNK):
        fs = slice(j * FF_CHUNK, (j + 1) * FF_CHUNK)
        u = jnp.maximum(_dot(hb, wup_ref[:, fs]), 0.0)
        f = f + _dot((u * u).astype(BF16), wdn_ref[fs, :])
    o_ref[...] = _layernorm(ALPHA * h + f, g2_ref[...], b2_ref[...])


def _out_call(x, sb, ssd, mla, w, tm):
    n = x.shape[0]
    row = lambda width: pl.BlockSpec((tm, width), lambda i: (i, 0))
    const = lambda shape: pl.BlockSpec(shape, lambda i: (0,) * len(shape), pipeline_mode=pl.Buffered(1))
    return pl.pallas_call(
        _out_body,
        grid=(n // tm,),
        in_specs=[row(D_MODEL), row(SB_W), row(SSD_INNER), row(MLA_W),
                  const((1, SB_W)), const((1, MLA_W)), const((D_MODEL, D_MODEL)),
                  const((1, D_MODEL)), const((1, D_MODEL)),
                  const((D_MODEL, D_FF)), const((D_FF, D_MODEL)),
                  const((1, D_MODEL)), const((1, D_MODEL))],
        out_specs=row(D_MODEL),
        out_shape=jax.ShapeDtypeStruct((n, D_MODEL), F32),
        compiler_params=_params(("arbitrary",)),
        name="out_ffn",
    )(x, sb, ssd, mla, w["sb_norm"], w["mla_norm"], w["w_out"], w["ln1_g"], w["ln1_b"],
      w["w_up"], w["w_down"], w["ln2_g"], w["ln2_b"])


def _rot_cols(wm):
    half = wm.shape[-1] // 2
    return jnp.concatenate([-wm[..., half:], wm[..., :half]], axis=-1)


def _layer_weights(l, w_in, sb_norm, conv_w, conv_b, dt_bias, a_log, d_skip, ssd_norm, q_norm, w_uq, kv_norm,
                   w_uk, w_uv, mla_norm, w_out, ln1_g, ln1_b, w_up, w_down, ln2_g, ln2_b):
    wi = w_in[l]
    o = [0, 256, 384, 512, 1024, 2048, 2056, 2312, 2568, 2600]
    sq, sk, sv, z, xbc, dtc, cq, ckv, kpe = [wi[:, o[i]:o[i + 1]] for i in range(9)]
    d = wi.shape[0]
    dt_pad = jnp.concatenate([dtc, jnp.zeros((d, LANE - SSD_HEADS), F32)], axis=1)
    kpe_grp = jnp.concatenate([kpe, _rot_cols(kpe), jnp.zeros((d, LANE - 2 * MLA_ROPE), F32)], axis=1)
    w1 = jnp.concatenate([sq, sk, sv, z, xbc, cq, ckv, dt_pad, kpe_grp], axis=1).astype(BF16)

    uq = w_uq[l]
    zq = lambda width: jnp.zeros((MLA_QR, MLA_HEADS, width), F32)
    pe = uq[..., MLA_NOPE:]
    w2a = jnp.concatenate([uq[..., :MLA_NOPE], pe, zq(HEAD_PAD - MLA_NOPE - MLA_ROPE)], axis=-1)
    w2b = jnp.concatenate([zq(MLA_NOPE), _rot_cols(pe), zq(HEAD_PAD - MLA_NOPE - MLA_ROPE)], axis=-1)
    w2 = jnp.concatenate([w2a.reshape(MLA_QR, -1), w2b.reshape(MLA_QR, -1)], axis=1).astype(BF16)

    uk = w_uk[l]
    wk_prompt = jnp.concatenate([uk, jnp.zeros((MLA_KVR, MLA_HEADS, HEAD_PAD - MLA_NOPE), F32)], axis=-1)
    wk_prompt = wk_prompt.reshape(MLA_KVR, MLA_HEADS * HEAD_PAD).astype(BF16)
    ukt = jnp.transpose(uk, (1, 2, 0))
    wk_decode = jnp.concatenate([ukt, jnp.zeros((MLA_HEADS, HEAD_PAD - MLA_NOPE, MLA_KVR), F32)], axis=1)
    wk_decode = wk_decode.reshape(MLA_HEADS * HEAD_PAD, MLA_KVR).astype(BF16)
    wv = w_uv[l].reshape(MLA_KVR, MLA_W).astype(BF16)

    pad_heads = lambda v: jnp.concatenate([v, jnp.zeros((LANE - SSD_HEADS,), F32)])[None, :]
    return {
        "w1": w1, "w2": w2, "wk_prompt": wk_prompt, "wk_decode": wk_decode, "wv": wv,
        "q_norm": q_norm[l][None, :], "kv_norm": kv_norm[l][None, :],
        "conv_w": conv_w[l], "conv_b": conv_b[l][None, :],
        "dt_bias": pad_heads(dt_bias[l]), "a_log": pad_heads(a_log[l]),
        "d_skip_e": jnp.repeat(d_skip[l], SSD_P)[None, :], "ssd_norm": ssd_norm[l][None, :],
        "sb_norm": sb_norm[l][None, :], "mla_norm": mla_norm[l][None, :],
        "w_out": w_out[l].astype(BF16), "ln1_g": ln1_g[l][None, :], "ln1_b": ln1_b[l][None, :],
        "w_up": w_up[l].astype(BF16), "w_down": w_down[l].astype(BF16),
        "ln2_g": ln2_g[l][None, :], "ln2_b": ln2_b[l][None, :],
    }


def _rope_table(pos):
    half = MLA_ROPE // 2
    inv = ROPE_THETA ** (-jnp.arange(half, dtype=F32) / half)
    ang = pos.astype(F32)[:, None] * inv[None, :]
    cos, sin = jnp.cos(ang), jnp.sin(ang)
    cos32 = jnp.concatenate([cos, cos], axis=1)
    sin32 = jnp.concatenate([sin, sin], axis=1)
    t = pos.shape[0]
    ones, zeros = jnp.ones, jnp.zeros
    cosq = jnp.concatenate([ones((t, MLA_NOPE), F32), cos32, zeros((t, 32), F32)], axis=1)
    sinq = jnp.concatenate([zeros((t, MLA_NOPE), F32), sin32, zeros((t, 32), F32)], axis=1)
    cosk = jnp.concatenate([cos32, zeros((t, 96), F32)], axis=1)
    sink = jnp.concatenate([sin32, zeros((t, 96), F32)], axis=1)
    return jnp.concatenate([cosq, sinq, cosk, sink], axis=1)


def _pick(n, candidates):
    for c in candidates:
        if n % c == 0:
            return c
    return n


def kernel(x_prompt, x_sample, cache_sb_k, cache_sb_v, cache_mla_ckv, cache_mla_kpe, state_ssm, state_conv,
           page_table, w_in, sb_norm, conv_w, conv_b, dt_bias, a_log, d_skip, ssd_norm, q_norm, w_uq, kv_norm,
           w_uk, w_uv, mla_norm, w_out, ln1_g, ln1_b, w_up, w_down, ln2_g, ln2_b):
    bp, seq, d = x_prompt.shape
    nd, dec_seq, _ = x_sample.shape
    assert dec_seq == 1 and d == D_MODEL
    depth = w_in.shape[0]
    n_pages = page_table.shape[1]
    past_len = n_pages * PAGE
    n_pool = cache_sb_k.shape[1]

    tab_p = _rope_table(jnp.arange(seq, dtype=jnp.int32))
    tab_s = _rope_table(jnp.full((nd,), past_len, dtype=jnp.int32))
    kc = jnp.transpose(cache_sb_k, (0, 1, 3, 4, 2)).reshape(depth, n_pool, SB_KV * SB_DIM, PAGE)
    vc = jnp.transpose(cache_sb_v, (0, 1, 3, 4, 2)).reshape(depth, n_pool, SB_KV * SB_DIM, PAGE)
    pc = jnp.transpose(cache_mla_kpe, (0, 1, 3, 2))

    tm_p = _pick(seq, (256, 128))
    tq = _pick(seq, (256, 128))
    chunk = _pick(seq, (128,))

    hp = x_prompt.reshape(bp * seq, d)
    hs = x_sample.reshape(nd, d)
    new_p, new_s = [], []
    for l in range(depth):
        w = _layer_weights(l, w_in, sb_norm, conv_w, conv_b, dt_bias, a_log, d_skip, ssd_norm, q_norm, w_uq,
                           kv_norm, w_uk, w_uv, mla_norm, w_out, ln1_g, ln1_b, w_up, w_down, ln2_g, ln2_b)
        wp = dict(w, wk=w["wk_prompt"])
        (sq, sk, sv, z, xbc, dt, ckv, kpe, kb, vb, qh, kh, vm) = _proj_call(hp, tab_p, wp, False, tm_p)
        sb_o = _sb_prompt_call(sq, kb, vb, bp, seq, tq)
        mla_o = _mla_prompt_call(qh, kh, vm, bp, seq, tq)
        ssd_o, ssm_new, conv_new = _ssd_prompt_call(z, xbc, dt, w, bp, seq, chunk)
        hp = _out_call(hp, sb_o, ssd_o, mla_o, w, tm_p)
        new_p.append((sk.reshape(bp, seq, SB_KV, SB_DIM), sv.reshape(bp, seq, SB_KV, SB_DIM),
                      ckv.reshape(bp, seq, MLA_KVR), kpe.reshape(bp, seq, MLA_ROPE),
                      ssm_new.reshape(bp, SSD_HEADS, SSD_P, SSD_N), conv_new))
        wd = dict(w, wk=w["wk_decode"])
        (sq, sk, sv, z, xbc, dt, ckv, kpe, qh, qlat) = _proj_call(hs, tab_s, wd, True, nd)
        q4 = sq.reshape(nd, SB_HEADS, SB_DIM)
        zq = jnp.zeros_like(q4)
        grp = (jnp.arange(SB_HEADS) // (SB_HEADS // SB_KV))[None, :, None]
        qbd = jnp.concatenate([jnp.where(grp == 0, q4, zq), jnp.where(grp == 1, q4, zq)], axis=-1)
        qbd = jnp.concatenate([qbd, jnp.zeros_like(qbd)], axis=1)
        sb_o = _sb_decode_call(page_table, qbd, kc, vc, l).reshape(nd, SB_W)
        pad8 = lambda a: jnp.concatenate([a, jnp.zeros_like(a)], axis=1)
        qlat8 = pad8(qlat.reshape(nd, MLA_HEADS, MLA_KVR))
        qpe8 = pad8(qh.reshape(nd, MLA_HEADS, HEAD_PAD)[:, :, MLA_NOPE:MLA_NOPE + MLA_ROPE])
        mla_o = _mla_decode_call(page_table, qlat8, qpe8, ckv[:, None, :], kpe[:, None, :], w["wv"],
                                 cache_mla_ckv, pc, l).reshape(nd, MLA_W)
        conv_t = jnp.transpose(state_conv[l], (1, 0, 2))
        h0 = state_ssm[l].reshape(nd, SSD_INNER, SSD_N)
        ssd_o, ssm_new, conv_new_t = _ssd_decode_call(z, xbc, dt, conv_t, h0, w)
        hs = _out_call(hs, sb_o, ssd_o, mla_o, w, nd)
        new_s.append((sk.reshape(nd, 1, SB_KV, SB_DIM), sv.reshape(nd, 1, SB_KV, SB_DIM),
                      ckv.reshape(nd, 1, MLA_KVR), kpe.reshape(nd, 1, MLA_ROPE),
                      ssm_new.reshape(nd, SSD_HEADS, SSD_P, SSD_N), jnp.transpose(conv_new_t, (1, 0, 2))))

    stack = lambda states, i: jnp.stack([st[i] for st in states])
    return (hp.reshape(bp, seq, d), hs.reshape(nd, 1, d),
            *[stack(new_p, i) for i in range(6)], *[stack(new_s, i) for i in range(6)])
```

```python
import functools
import math

import jax
import jax.numpy as jnp
from jax import lax
from jax.experimental import pallas as pl
from jax.experimental.pallas import tpu as pltpu

F32 = jnp.float32
BF16 = jnp.bfloat16

D_MODEL = 1024
PAGE = 128
SB_HEADS = 4
SB_KV = 2
SB_DIM = 64
SB_W = SB_HEADS * SB_DIM
SSD_HEADS = 8
SSD_P = 64
SSD_INNER = SSD_HEADS * SSD_P
SSD_GROUPS = 2
SSD_N = 128
SSD_CONV = 4
SSD_CH = SSD_INNER + 2 * SSD_GROUPS * SSD_N
MLA_HEADS = 4
MLA_NOPE = 64
MLA_ROPE = 32
MLA_V = 64
MLA_QR = 256
MLA_KVR = 256
MLA_W = MLA_HEADS * MLA_V
MLA_SCALE = (MLA_NOPE + MLA_ROPE) ** -0.5
ROPE_THETA = 10000.0
D_FF = 4 * D_MODEL
EPS = 1e-5
DEPTH = 2
ALPHA = (2 * DEPTH) ** 0.25
LANE = 128
HEAD_PAD = 128

SB_DEAD = -104.0

VMEM_LIMIT = 56 * 1024 * 1024

C_SQ, C_SK, C_SV, C_Z, C_XBC, C_CQ, C_CKV, C_DT, C_KPE, C_END = (
    0, 256, 384, 512, 1024, 2048, 2304, 2560, 2688, 2816)


def _dot(a, b):
    return jnp.dot(a, b, preferred_element_type=F32)


def _dot_nt(a, b):
    return lax.dot_general(a, b, (((1,), (1,)), ((), ())), preferred_element_type=F32)


def _dot_tn(a, b):
    return lax.dot_general(a, b, (((0,), (0,)), ((), ())), preferred_element_type=F32)


def _split_hi_lo(x):
    hi = x.astype(BF16)
    lo = (x - hi.astype(F32)).astype(BF16)
    return hi, lo


def _dot_f32ish(x, w01):
    hi, lo = _split_hi_lo(x)
    return _dot(hi, w01) + _dot(lo, w01)


def _rms(x, g):
    return x * lax.rsqrt(jnp.mean(x * x, axis=-1, keepdims=True) + EPS) * g


def _layernorm(x, g, b):
    mu = jnp.mean(x, axis=-1, keepdims=True)
    xc = x - mu
    var = jnp.mean(xc * xc, axis=-1, keepdims=True)
    return xc * lax.rsqrt(var + EPS) * g + b


def _softplus(x):
    return jnp.maximum(x, 0.0) + jnp.log1p(jnp.exp(-jnp.abs(x)))


def _silu(x):
    return x / (1.0 + jnp.exp(-x))


def _full(shape):
    nd = len(shape)
    return pl.BlockSpec(shape, lambda *_: (0,) * nd)


def _params(sem):
    return pltpu.CompilerParams(dimension_semantics=sem, vmem_limit_bytes=VMEM_LIMIT)


def _proj_body(decode, x_ref, tab_ref, w1_ref, w2_ref, wk_ref, wv_ref, qg_ref, kvg_ref, *outs):
    if decode:
        (sq_ref, sk_ref, sv_ref, z_ref, xbc_ref, dt_ref, ckv_ref, kpe_ref, qh_ref, qlat_ref) = outs
    else:
        (sq_ref, sk_ref, sv_ref, z_ref, xbc_ref, dt_ref, ckv_ref, kpe_ref,
         kb_ref, vb_ref, qh_ref, kh_ref, vm_ref) = outs
    x = x_ref[...].astype(BF16)

    def col(lo, hi):
        return _dot(x, w1_ref[:, lo:hi])

    sq_ref[...] = col(C_SQ, C_SK)
    sk = col(C_SK, C_SV)
    sv = col(C_SV, C_Z)
    sk_ref[...] = sk
    sv_ref[...] = sv
    z_ref[...] = col(C_Z, C_XBC)
    xbc_ref[...] = col(C_XBC, C_CQ)
    dt_ref[...] = col(C_DT, C_KPE)

    cosq = tab_ref[:, 0:128]
    sinq = tab_ref[:, 128:256]
    cosk = tab_ref[:, 256:384]
    sink = tab_ref[:, 384:512]

    cqn = _rms(col(C_CQ, C_CKV), qg_ref[...]).astype(BF16)
    qa = _dot(cqn, w2_ref[:, 0:512])
    qb = _dot(cqn, w2_ref[:, 512:1024])
    qh = []
    for h in range(MLA_HEADS):
        sl = slice(h * HEAD_PAD, (h + 1) * HEAD_PAD)
        qh.append((qa[:, sl] * cosq + qb[:, sl] * sinq) * MLA_SCALE)

    ckvn = _rms(col(C_CKV, C_DT), kvg_ref[...])
    ckv_ref[...] = ckvn
    kg = col(C_KPE, C_END)
    kpe = kg * cosk + pltpu.roll(kg, 96, 1) * sink
    kpe_ref[...] = kpe[:, 0:MLA_ROPE]

    if decode:
        for h in range(MLA_HEADS):
            sl = slice(h * HEAD_PAD, (h + 1) * HEAD_PAD)
            qh_ref[:, sl] = qh[h]
            qlat_ref[:, h * MLA_KVR:(h + 1) * MLA_KVR] = _dot(qh[h].astype(BF16), wk_ref[sl, :])
    else:
        kb_ref[...] = sk.astype(BF16)
        vb_ref[...] = sv.astype(BF16)
        ckvb = ckvn.astype(BF16)
        kpe_at64 = pltpu.roll(kpe, 64, 1)
        for h in range(MLA_HEADS):
            sl = slice(h * HEAD_PAD, (h + 1) * HEAD_PAD)
            qh_ref[:, sl] = qh[h].astype(BF16)
            kh_ref[:, sl] = (_dot(ckvb, wk_ref[:, sl]) + kpe_at64).astype(BF16)
        vm_ref[0] = _dot_nt(wv_ref[...], ckvb).astype(BF16)


def _proj_call(x, tab, w, decode, tm):
    n = x.shape[0]
    nt = tab.shape[0] // tm
    grid = (n // tm,)
    row = lambda width: pl.BlockSpec((tm, width), lambda i: (i, 0))
    in_specs = [
        row(D_MODEL),
        pl.BlockSpec((tm, 512), lambda i: (i % nt, 0)),
        _full(w["w1"].shape), _full(w["w2"].shape), _full(w["wk"].shape), _full(w["wv"].shape),
        _full((1, MLA_QR)), _full((1, MLA_KVR)),
    ]
    f = lambda width, dt=F32: jax.ShapeDtypeStruct((n, width), dt)
    out_shape = [f(256), f(128), f(128), f(512), f(1024), f(128), f(256), f(MLA_ROPE)]
    out_specs = [row(256), row(128), row(128), row(512), row(1024), row(128), row(256), row(MLA_ROPE)]
    if decode:
        out_shape += [f(512), f(MLA_HEADS * MLA_KVR)]
        out_specs += [row(512), row(MLA_HEADS * MLA_KVR)]
    else:
        out_shape += [f(128, BF16), f(128, BF16), f(512, BF16), f(512, BF16),
                      jax.ShapeDtypeStruct((n // tm, MLA_W, tm), BF16)]
        out_specs += [row(128), row(128), row(512), row(512),
                      pl.BlockSpec((1, MLA_W, tm), lambda i: (i, 0, 0))]
    return pl.pallas_call(
        functools.partial(_proj_body, decode),
        grid=grid, in_specs=in_specs, out_specs=out_specs, out_shape=out_shape,
        compiler_params=_params(("arbitrary",)),
        name="proj_decode" if decode else "proj_prompt",
    )(x, tab, w["w1"], w["w2"], w["wk"], w["wv"], w["q_norm"], w["kv_norm"])


def _suffix_matrix(tk):
    j = lax.broadcasted_iota(jnp.int32, (tk, tk), 0)
    s = lax.broadcasted_iota(jnp.int32, (tk, tk), 1)
    return jnp.where(j > s, 1.0, 0.0).astype(BF16)


def _sb_block(qall, kblk, vblk, umat, c, mask, keys_on_lanes=False):
    z = (_dot(qall, kblk) if keys_on_lanes else _dot_nt(qall, kblk)) * (SB_DIM ** -0.5)
    lse = jnp.log1p(jnp.exp(-jnp.abs(z)))
    log_beta = jnp.minimum(z, 0.0) - lse
    log_keep = log_beta - z
    if mask is not None:
        log_keep = jnp.where(mask, log_keep, 0.0)
    hi, lo = _split_hi_lo(log_keep)
    after = _dot(hi, umat) + _dot(lo, umat)
    w = jnp.exp(log_beta + after + c)
    if mask is not None:
        w = jnp.where(mask, w, 0.0)
    wb = w.astype(BF16)
    pv = _dot_nt(wb, vblk) if keys_on_lanes else _dot(wb, vblk)
    return pv, c + jnp.sum(log_keep, axis=1, keepdims=True)


def _sb_prompt_body(q_ref, k_ref, v_ref, o_ref, qall_ref, acc_ref, c_ref, *, tq):
    i = pl.program_id(1)
    tk = tq
    lane = lax.broadcasted_iota(jnp.int32, (tq, LANE), 1)
    lo_half = lane < SB_DIM
    s0 = q_ref[:, 0:128]
    s1 = q_ref[:, 128:256]
    qall_ref[0 * tq:1 * tq, :] = jnp.where(lo_half, s0, 0.0).astype(BF16)
    qall_ref[1 * tq:2 * tq, :] = jnp.where(lo_half, pltpu.roll(s0, 64, 1), 0.0).astype(BF16)
    qall_ref[2 * tq:3 * tq, :] = jnp.where(lo_half, 0.0, pltpu.roll(s1, 64, 1)).astype(BF16)
    qall_ref[3 * tq:4 * tq, :] = jnp.where(lo_half, 0.0, s1).astype(BF16)
    umat = _suffix_matrix(tk)

    row = lax.rem(lax.broadcasted_iota(jnp.int32, (SB_HEADS * tq, tk), 0), tq)
    colm = lax.broadcasted_iota(jnp.int32, (SB_HEADS * tq, tk), 1)
    mask = colm < row

    def run(kb, c, m):
        start = pl.multiple_of(kb * tk, tk)
        return _sb_block(qall_ref[...], k_ref[pl.ds(start, tk), :], v_ref[pl.ds(start, tk), :], umat, c, m)

    pv, c = run(i, jnp.zeros((SB_HEADS * tq, 1), F32), mask)
    acc_ref[...] = pv
    c_ref[...] = c

    def cond(st):
        kb, cmax = st
        return jnp.logical_and(kb >= 0, cmax > SB_DEAD)

    def body(st):
        kb, _ = st
        pv, c = run(kb, c_ref[...], None)
        acc_ref[...] += pv
        c_ref[...] = c
        return kb - 1, jnp.max(c)

    lax.while_loop(cond, body, (i - 1, jnp.max(c)))

    o0 = acc_ref[0 * tq:1 * tq, :]
    o1 = acc_ref[1 * tq:2 * tq, :]
    o2 = acc_ref[2 * tq:3 * tq, :]
    o3 = acc_ref[3 * tq:4 * tq, :]
    o_ref[:, 0:128] = jnp.where(lo_half, o0, pltpu.roll(o1, 64, 1))
    o_ref[:, 128:256] = jnp.where(lo_half, pltpu.roll(o2, 64, 1), o3)


def _sb_prompt_call(sq, kb, vb, bsz, seq, tq):
    nq = seq // tq
    return pl.pallas_call(
        functools.partial(_sb_prompt_body, tq=tq),
        grid=(bsz, nq),
        in_specs=[
            pl.BlockSpec((tq, SB_W), lambda b, i: (b * nq + i, 0)),
            pl.BlockSpec((seq, 128), lambda b, i: (b, 0)),
            pl.BlockSpec((seq, 128), lambda b, i: (b, 0)),
        ],
        out_specs=pl.BlockSpec((tq, SB_W), lambda b, i: (b * nq + i, 0)),
        out_shape=jax.ShapeDtypeStruct((bsz * seq, SB_W), F32),
        scratch_shapes=[
            pltpu.VMEM((SB_HEADS * tq, LANE), BF16),
            pltpu.VMEM((SB_HEADS * tq, LANE), F32),
            pltpu.VMEM((SB_HEADS * tq, 1), F32),
        ],
        compiler_params=_params(("arbitrary", "arbitrary")),
        name="sb_prompt",
    )(sq, kb, vb)


SB_DEC_PAGES = 2


def _sb_decode_body(pt_ref, q_ref, kc_ref, vc_ref, o_ref, kbuf, vbuf, sem, *, layer, n_pages):
    s = pl.program_id(0)
    g_pages = SB_DEC_PAGES
    ng = n_pages // g_pages
    tk = g_pages * PAGE

    def copies(seq, g, slot):
        out = []
        for j in range(g_pages):
            page = pt_ref[seq, g * g_pages + j]
            dst = pl.ds(j * PAGE, PAGE)
            out.append(pltpu.make_async_copy(kc_ref.at[layer, page], kbuf.at[slot, :, dst], sem.at[0, slot]))
            out.append(pltpu.make_async_copy(vc_ref.at[layer, page], vbuf.at[slot, :, dst], sem.at[1, slot]))
        return out

    def start(seq, g, slot):
        for cp in copies(seq, g, slot):
            cp.start()

    def wait(seq, g, slot):
        for cp in copies(seq, g, slot):
            cp.wait()

    newest_slot = lambda seq: 2 + lax.rem(seq, 2)
    slot_of = lambda g: lax.rem(ng - 2 - g, 2)
    umat = _suffix_matrix(tk)
    qall = q_ref[0].astype(BF16)

    def step(slot, c, acc):
        pv, c = _sb_block(qall, kbuf[slot].astype(BF16), vbuf[slot].astype(BF16), umat, c, None,
                          keys_on_lanes=True)
        return jnp.max(c[0:SB_HEADS]), c, acc + pv

    @pl.when(s == 0)
    def _():
        start(s, ng - 1, newest_slot(s))

    @pl.when(s + 1 < pl.num_programs(0))
    def _():
        start(s + 1, ng - 1, newest_slot(s + 1))

    if ng >= 2:
        start(s, ng - 2, 0)
    wait(s, ng - 1, newest_slot(s))
    first = step(newest_slot(s), jnp.zeros((8, 1), F32), jnp.zeros((8, LANE), F32))

    def cond(st):
        g, cmax, _, _ = st
        return jnp.logical_and(g >= 0, cmax > SB_DEAD)

    def body(st):
        g, _, c, acc = st
        slot = slot_of(g)

        @pl.when(g >= 1)
        def _():
            start(s, g - 1, 1 - slot)

        wait(s, g, slot)
        return (g - 1, *step(slot, c, acc))

    g_end, _, _, acc = lax.while_loop(cond, body, (jnp.int32(ng - 2), *first))

    @pl.when(g_end >= 0)
    def _():
        wait(s, g_end, slot_of(g_end))

    lane = lax.broadcasted_iota(jnp.int32, (1, LANE), 1)
    lo_half = lane < SB_DIM
    o_ref[0, :, 0:128] = jnp.where(lo_half, acc[0:1], pltpu.roll(acc[1:2], 64, 1))
    o_ref[0, :, 128:256] = jnp.where(lo_half, pltpu.roll(acc[2:3], 64, 1), acc[3:4])


def _sb_decode_call(page_table, qbd, kc, vc, layer):
    n, n_pages = page_table.shape
    grid_spec = pltpu.PrefetchScalarGridSpec(
        num_scalar_prefetch=1,
        grid=(n,),
        in_specs=[
            pl.BlockSpec((1, 8, LANE), lambda s, pt: (s, 0, 0)),
            pl.BlockSpec(memory_space=pl.ANY),
            pl.BlockSpec(memory_space=pl.ANY),
        ],
        out_specs=pl.BlockSpec((1, 1, SB_W), lambda s, pt: (s, 0, 0)),
        scratch_shapes=[
            pltpu.VMEM((4, LANE, SB_DEC_PAGES * PAGE), F32),
            pltpu.VMEM((4, LANE, SB_DEC_PAGES * PAGE), F32),
            pltpu.SemaphoreType.DMA((2, 4)),
        ],
    )
    return pl.pallas_call(
        functools.partial(_sb_decode_body, layer=layer, n_pages=n_pages),
        grid_spec=grid_spec,
        out_shape=jax.ShapeDtypeStruct((n, 1, SB_W), F32),
        compiler_params=_params(("arbitrary",)),
        name="sb_decode",
    )(page_table, qbd, kc, vc)


def _mla_prompt_body(q_ref, k_ref, vt_ref, o_ref, *, tq, tk):
    i = pl.program_id(1)
    ratio = tq // tk
    key = lax.broadcasted_iota(jnp.int32, (tk, tq), 0)
    qry = lax.broadcasted_iota(jnp.int32, (tk, tq), 1)

    def block(kb, state, mask):
        start = pl.multiple_of(kb * tk, tk)
        heads = range(MLA_HEADS)
        sts = []
        for h in heads:
            sl = slice(h * HEAD_PAD, (h + 1) * HEAD_PAD)
            sts.append(_dot_nt(k_ref[pl.ds(start, tk), sl], q_ref[:, sl]))
        ps, ms, ls, alphas = [], [], [], []
        for h in heads:
            m_old, l_old, _ = state[h]
            st = sts[h] if mask is None else jnp.where(mask, sts[h], -jnp.inf)
            m_new = jnp.maximum(m_old, jnp.max(st, axis=0, keepdims=True))
            p = jnp.exp(st - m_new)
            alpha = jnp.exp(m_old - m_new)
            ls.append(alpha * l_old + jnp.sum(p, axis=0, keepdims=True))
            ms.append(m_new)
            alphas.append(alpha)
            ps.append(p.astype(BF16))
        pvs = [_dot(vt_ref[kb, h * MLA_V:(h + 1) * MLA_V, :], ps[h]) for h in heads]
        return tuple((ms[h], ls[h], alphas[h] * state[h][2] + pvs[h]) for h in heads)

    init = tuple((jnp.full((1, tq), -jnp.inf, F32), jnp.zeros((1, tq), F32), jnp.zeros((MLA_V, tq), F32))
                 for _ in range(MLA_HEADS))
    state = lax.fori_loop(0, i * ratio, lambda kb, st: block(kb, st, None), init)
    for j in range(ratio):
        state = block(i * ratio + j, state, key + j * tk <= qry)
    o_t = jnp.concatenate([acc / l for (_, l, acc) in state], axis=0)
    o_ref[...] = o_t.T


def _mla_prompt_call(qh, kh, vmt, bsz, seq, tq, tk):
    nq = seq // tq
    nk = seq // tk
    return pl.pallas_call(
        functools.partial(_mla_prompt_body, tq=tq, tk=tk),
        grid=(bsz, nq),
        in_specs=[
            pl.BlockSpec((tq, 512), lambda b, i: (b * nq + i, 0)),
            pl.BlockSpec((seq, 512), lambda b, i: (b, 0)),
            pl.BlockSpec((nk, MLA_W, tk), lambda b, i: (b, 0, 0)),
        ],
        out_specs=pl.BlockSpec((tq, MLA_W), lambda b, i: (b * nq + i, 0)),
        out_shape=jax.ShapeDtypeStruct((bsz * seq, MLA_W), F32),
        compiler_params=_params(("arbitrary", "arbitrary")),
        name="mla_prompt",
    )(qh, kh, vmt)


MLA_DEC_PAGES = 8
MLA_DEC_BUFS = 4


def _mla_decode_body(pt_ref, ql_ref, qp_ref, cn_ref, pn_ref, wuv_ref, cc_ref, pc_ref, o_ref,
                     cbuf, pbuf, cb16, sem, *, layer, n_pages):
    s = pl.program_id(0)
    g_pages = MLA_DEC_PAGES
    ng = n_pages // g_pages
    nbuf = MLA_DEC_BUFS
    total = pl.num_programs(0) * ng

    def copies(t):
        seq = lax.div(t, jnp.int32(ng))
        g = t - seq * ng
        slot = lax.rem(t, jnp.int32(nbuf))
        out = []
        for j in range(g_pages):
            page = pt_ref[seq, g * g_pages + j]
            dst = pl.ds(j * PAGE, PAGE)
            out.append(pltpu.make_async_copy(cc_ref.at[layer, page], cbuf.at[slot, dst], sem.at[0, slot]))
            out.append(pltpu.make_async_copy(pc_ref.at[layer, page], pbuf.at[slot, :, dst], sem.at[1, slot]))
        return out

    def start(t):
        for cp in copies(t):
            cp.start()

    def wait(t):
        for cp in copies(t):
            cp.wait()

    ql = ql_ref[0]
    qp = qp_ref[0]
    qlb = ql.astype(BF16)
    qpb = qp.astype(BF16)

    @pl.when(s == 0)
    def _():
        for t0 in range(nbuf):
            start(t0)

    def scores(t):
        wait(t)
        slot = lax.rem(t, jnp.int32(nbuf))
        cb = cbuf[slot].astype(BF16)
        cb16[lax.rem(t, jnp.int32(2))] = cb
        return _dot_nt(qlb, cb) + _dot(qpb, pbuf[slot].astype(BF16))

    def refill(t):
        @pl.when(t + nbuf < total)
        def _():
            start(t + nbuf)

    def absorb(t, sc, st):
        m_old, l_old, acc = st
        m_new = jnp.maximum(m_old, jnp.max(sc, axis=1, keepdims=True))
        p = jnp.exp(sc - m_new)
        alpha = jnp.exp(m_old - m_new)
        pv = _dot(p.astype(BF16), cb16[lax.rem(t, jnp.int32(2))])
        return m_new, alpha * l_old + jnp.sum(p, axis=1, keepdims=True), alpha * acc + pv

    t_first = s * ng

    def body(g, carry):
        sc, st = carry
        t = t_first + g
        refill(t)
        sc_next = scores(t + 1)
        return sc_next, absorb(t, sc, st)

    init = (jnp.full((8, 1), -jnp.inf, F32), jnp.zeros((8, 1), F32), jnp.zeros((8, MLA_KVR), F32))
    sc, st = lax.fori_loop(0, ng - 1, body, (scores(t_first), init))
    refill(t_first + ng - 1)
    m_old, l_old, acc = absorb(t_first + ng - 1, sc, st)

    cn = cn_ref[0]
    pn = pn_ref[0]
    sc = jnp.sum(ql * cn, axis=1, keepdims=True) + jnp.sum(qp * pn, axis=1, keepdims=True)
    m_new = jnp.maximum(m_old, sc)
    p = jnp.exp(sc - m_new)
    alpha = jnp.exp(m_old - m_new)
    l_new = alpha * l_old + p
    o_lat = (alpha * acc + p * cn) / l_new
    o_all = _dot(o_lat.astype(BF16), wuv_ref[...])
    sub = lax.broadcasted_iota(jnp.int32, (8, MLA_W), 0)
    lane = lax.broadcasted_iota(jnp.int32, (8, MLA_W), 1)
    pick = (lane // MLA_V) == sub
    o_ref[0] = jnp.sum(jnp.where(pick, o_all, 0.0), axis=0, keepdims=True)


def _mla_decode_call(page_table, qlat, qpe, ckv_new, kpe_new, wuv, cc, pc, layer):
    n, n_pages = page_table.shape
    tk = MLA_DEC_PAGES * PAGE
    grid_spec = pltpu.PrefetchScalarGridSpec(
        num_scalar_prefetch=1,
        grid=(n,),
        in_specs=[
            pl.BlockSpec((1, 8, MLA_KVR), lambda s, pt: (s, 0, 0)),
            pl.BlockSpec((1, 8, MLA_ROPE), lambda s, pt: (s, 0, 0)),
            pl.BlockSpec((1, 1, MLA_KVR), lambda s, pt: (s, 0, 0)),
            pl.BlockSpec((1, 1, MLA_ROPE), lambda s, pt: (s, 0, 0)),
            pl.BlockSpec((MLA_KVR, MLA_W), lambda s, pt: (0, 0)),
            pl.BlockSpec(memory_space=pl.ANY),
            pl.BlockSpec(memory_space=pl.ANY),
        ],
        out_specs=pl.BlockSpec((1, 1, MLA_W), lambda s, pt: (s, 0, 0)),
        scratch_shapes=[
            pltpu.VMEM((MLA_DEC_BUFS, tk, MLA_KVR), F32),
            pltpu.VMEM((MLA_DEC_BUFS, MLA_ROPE, tk), F32),
            pltpu.VMEM((2, tk, MLA_KVR), BF16),
            pltpu.SemaphoreType.DMA((2, MLA_DEC_BUFS)),
        ],
    )
    assert n_pages % MLA_DEC_PAGES == 0 and n * (n_pages // MLA_DEC_PAGES) >= MLA_DEC_BUFS
    return pl.pallas_call(
        functools.partial(_mla_decode_body, layer=layer, n_pages=n_pages),
        grid_spec=grid_spec,
        out_shape=jax.ShapeDtypeStruct((n, 1, MLA_W), F32),
        compiler_params=_params(("arbitrary",)),
        name="mla_decode",
    )(page_table, qlat, qpe, ckv_new, kpe_new, wuv, cc, pc)


def _expand_matrix():
    r = lax.broadcasted_iota(jnp.int32, (LANE, SSD_INNER), 0)
    c = lax.broadcasted_iota(jnp.int32, (LANE, SSD_INNER), 1)
    return jnp.where(c // SSD_P == r, 1.0, 0.0).astype(BF16)


def _gated_norm(y, z, norm_w):
    g = y * _silu(z)
    half = SSD_INNER // SSD_GROUPS
    parts = []
    for k in range(SSD_GROUPS):
        gk = g[:, k * half:(k + 1) * half]
        parts.append(gk * lax.rsqrt(jnp.mean(gk * gk, axis=-1, keepdims=True) + EPS))
    return jnp.concatenate(parts, axis=1) * norm_w


def _ssd_prompt_body(z_ref, xbc_ref, dt_ref, cw_ref, cb_ref, dtb_ref, alog_ref, dskip_ref, nw_ref,
                     y_ref, hout_ref, cout_ref, xwin, ht, *, chunk):
    c = pl.program_id(1)
    nc = pl.num_programs(1)
    L = chunk

    @pl.when(c == 0)
    def _():
        xwin[0:8, :] = jnp.zeros((8, SSD_CH), F32)
        ht[...] = jnp.zeros(ht.shape, F32)

    xwin[8:8 + L, :] = xbc_ref[...]
    conv = cb_ref[...] + xwin[5:5 + L, :] * cw_ref[0:1, :]
    for i in range(1, SSD_CONV):
        conv = conv + xwin[5 + i:5 + i + L, :] * cw_ref[i:i + 1, :]
    xwin[0:8, :] = xwin[L:L + 8, :]
    cv = _silu(conv)
    xs = cv[:, 0:SSD_INNER]
    bm = cv[:, SSD_INNER:SSD_INNER + 256]
    cm = cv[:, SSD_INNER + 256:SSD_INNER + 512]

    dt = _softplus(dt_ref[...] + dtb_ref[...])
    da = dt * (-jnp.exp(alog_ref[...]))
    r = lax.broadcasted_iota(jnp.int32, (L, L), 0)
    s = lax.broadcasted_iota(jnp.int32, (L, L), 1)
    causal = s <= r
    tri = jnp.where(causal, 1.0, 0.0).astype(BF16)
    cs = _dot_f32ish_left(tri, da)
    cs_t = cs.T
    cs_last = cs[L - 1:L, :]
    ecs = jnp.exp(cs)
    emat = _expand_matrix()
    dt_e = _dot_f32ish(dt, emat)
    ecs_e = _dot_f32ish(ecs, emat)
    te_e = _dot_f32ish(dt * jnp.exp(cs_last - cs), emat)
    last_e = _dot_f32ish(jnp.exp(cs_last), emat)

    x_dt = (xs * dt_e).astype(BF16)
    x_end = (xs * te_e).astype(BF16)
    lane = lax.broadcasted_iota(jnp.int32, (L, LANE), 1)
    lo_half = lane < SSD_P
    y_parts = []
    for g in range(SSD_GROUPS):
        gs = slice(g * SSD_N, (g + 1) * SSD_N)
        cg = cm[:, gs].astype(BF16)
        bg = bm[:, gs].astype(BF16)
        cb = _dot_nt(cg, bg)
        hs = slice(g * 256, (g + 1) * 256)
        h_in = ht[g]
        y_off = _dot(cg, h_in.astype(BF16)) * ecs_e[:, hs]
        yd = []
        for hh in range(4):
            h = g * 4 + hh
            seg = cs[:, h:h + 1] - cs_t[h:h + 1, :]
            decay = jnp.exp(jnp.where(causal, seg, -jnp.inf))
            sc = (cb * decay).astype(BF16)
            ps = slice((h // 2) * LANE, (h // 2 + 1) * LANE)
            yd.append(_dot(sc, x_dt[:, ps]))
        y_diag = jnp.concatenate([jnp.where(lo_half, yd[0], yd[1]), jnp.where(lo_half, yd[2], yd[3])], axis=1)
        y_parts.append(y_diag + y_off)
        ht[g] = h_in * last_e[:, hs] + _dot_tn(bg, x_end[:, hs])
    y = jnp.concatenate(y_parts, axis=1) + dskip_ref[...] * xs
    y_ref[...] = _gated_norm(y, z_ref[...], nw_ref[...])

    @pl.when(c == nc - 1)
    def _():
        for g in range(SSD_GROUPS):
            hout_ref[0, g * 256:(g + 1) * 256, :] = ht[g].T
        cout_ref[0] = xwin[0:8, :][5:8, :]


def _dot_f32ish_left(w01, x):
    hi, lo = _split_hi_lo(x)
    return _dot(w01, hi) + _dot(w01, lo)


def _ssd_prompt_call(z, xbc, dt, w, bsz, seq, chunk):
    nc = seq // chunk
    row = lambda width: pl.BlockSpec((chunk, width), lambda b, c: (b * nc + c, 0))
    return pl.pallas_call(
        functools.partial(_ssd_prompt_body, chunk=chunk),
        grid=(bsz, nc),
        in_specs=[row(SSD_INNER), row(SSD_CH), row(LANE),
                  _full((SSD_CONV, SSD_CH)), _full((1, SSD_CH)), _full((1, LANE)), _full((1, LANE)),
                  _full((1, SSD_INNER)), _full((1, SSD_INNER))],
        out_specs=[row(SSD_INNER),
                   pl.BlockSpec((1, SSD_INNER, SSD_N), lambda b, c: (b, 0, 0)),
                   pl.BlockSpec((1, SSD_CONV - 1, SSD_CH), lambda b, c: (b, 0, 0))],
        out_shape=[jax.ShapeDtypeStruct((bsz * seq, SSD_INNER), F32),
                   jax.ShapeDtypeStruct((bsz, SSD_INNER, SSD_N), F32),
                   jax.ShapeDtypeStruct((bsz, SSD_CONV - 1, SSD_CH), F32)],
        scratch_shapes=[pltpu.VMEM((chunk + 8, SSD_CH), F32), pltpu.VMEM((SSD_GROUPS, SSD_N, 256), F32)],
        compiler_params=_params(("arbitrary", "arbitrary")),
        name="ssd_prompt",
    )(z, xbc, dt, w["conv_w"], w["conv_b"], w["dt_bias"], w["a_log"], w["d_skip_e"], w["ssd_norm"])


def _ssd_decode_body(z_ref, xbc_ref, dt_ref, cbuf_ref, h0_ref, cw_ref, cb_ref, dtb_ref, alog_ref, dskip_ref,
                     nw_ref, y_ref, hout_ref, cout_ref, xt, dect, bsel, csel, yt, xs_keep):
    s = pl.program_id(0)
    n = pl.num_programs(0)

    @pl.when(s == 0)
    def _():
        xnew = xbc_ref[...]
        conv = cb_ref[...] + xnew * cw_ref[3:4, :]
        for i in range(SSD_CONV - 1):
            conv = conv + cbuf_ref[i] * cw_ref[i:i + 1, :]
        cout_ref[0] = cbuf_ref[1]
        cout_ref[1] = cbuf_ref[2]
        cout_ref[2] = xnew
        cv = _silu(conv)
        xs = cv[:, 0:SSD_INNER]
        xs_keep[...] = xs
        bsel[...] = cv[:, SSD_INNER:SSD_INNER + 256]
        csel[...] = cv[:, SSD_INNER + 256:SSD_INNER + 512]
        dt = _softplus(dt_ref[...] + dtb_ref[...])
        da = dt * (-jnp.exp(alog_ref[...]))
        emat = _expand_matrix()
        xt[...] = (xs * _dot_f32ish(dt, emat)).T
        dect[...] = _dot_f32ish(jnp.exp(da), emat).T
        yt[...] = jnp.zeros(yt.shape, F32)

    lane = lax.broadcasted_iota(jnp.int32, xt.shape, 1)
    here = lane == s
    x_col = jnp.sum(jnp.where(here, xt[...], 0.0), axis=1, keepdims=True)
    d_col = jnp.sum(jnp.where(here, dect[...], 0.0), axis=1, keepdims=True)
    b_row = bsel[pl.ds(s, 1), :]
    c_row = csel[pl.ds(s, 1), :]
    ys = []
    for g in range(SSD_GROUPS):
        hs = slice(g * 256, (g + 1) * 256)
        gs = slice(g * SSD_N, (g + 1) * SSD_N)
        h_new = h0_ref[0, hs, :] * d_col[hs] + x_col[hs] * b_row[:, gs]
        hout_ref[0, hs, :] = h_new
        ys.append(jnp.sum(h_new * c_row[:, gs], axis=1, keepdims=True))
    y_col = jnp.concatenate(ys, axis=0)
    yt[...] = jnp.where(here, y_col, yt[...])

    @pl.when(s == n - 1)
    def _():
        xs = xs_keep[...]
        y = yt[...].T + dskip_ref[...] * xs
        y_ref[...] = _gated_norm(y, z_ref[...], nw_ref[...])


def _ssd_decode_call(z, xbc, dt, conv_t, h0, w):
    n = z.shape[0]
    return pl.pallas_call(
        _ssd_decode_body,
        grid=(n,),
        in_specs=[_full((n, SSD_INNER)), _full((n, SSD_CH)), _full((n, LANE)),
                  _full((SSD_CONV - 1, n, SSD_CH)),
                  pl.BlockSpec((1, SSD_INNER, SSD_N), lambda s: (s, 0, 0)),
                  _full((SSD_CONV, SSD_CH)), _full((1, SSD_CH)), _full((1, LANE)), _full((1, LANE)),
                  _full((1, SSD_INNER)), _full((1, SSD_INNER))],
        out_specs=[_full((n, SSD_INNER)),
                   pl.BlockSpec((1, SSD_INNER, SSD_N), lambda s: (s, 0, 0)),
                   _full((SSD_CONV - 1, n, SSD_CH))],
        out_shape=[jax.ShapeDtypeStruct((n, SSD_INNER), F32),
                   jax.ShapeDtypeStruct((n, SSD_INNER, SSD_N), F32),
                   jax.ShapeDtypeStruct((SSD_CONV - 1, n, SSD_CH), F32)],
        scratch_shapes=[pltpu.VMEM((SSD_INNER, n), F32), pltpu.VMEM((SSD_INNER, n), F32),
                        pltpu.VMEM((n, 256), F32), pltpu.VMEM((n, 256), F32),
                        pltpu.VMEM((SSD_INNER, n), F32), pltpu.VMEM((n, SSD_INNER), F32)],
        compiler_params=_params(("arbitrary",)),
        name="ssd_decode",
    )(z, xbc, dt, conv_t, h0, w["conv_w"], w["conv_b"], w["dt_bias"], w["a_log"], w["d_skip_e"], w["ssd_norm"])


FF_CHUNK = 1024


def _out_body(x_ref, sb_ref, ssd_ref, mla_ref, sbg_ref, mlag_ref, wo_ref, g1_ref, b1_ref,
              wup_ref, wdn_ref, g2_ref, b2_ref, o_ref):
    sbn = _rms(sb_ref[...], sbg_ref[...]).astype(BF16)
    mlan = _rms(mla_ref[...], mlag_ref[...]).astype(BF16)
    y = (_dot(sbn, wo_ref[0:SB_W, :])
         + _dot(ssd_ref[...].astype(BF16), wo_ref[SB_W:SB_W + SSD_INNER, :])
         + _dot(mlan, wo_ref[SB_W + SSD_INNER:, :]))
    h = _layernorm(ALPHA * x_ref[...] + y, g1_ref[...], b1_ref[...])
    hb = h.astype(BF16)
    f = jnp.zeros(h.shape, F32)
    for j in range(D_FF // FF_CHUNK):
        fs = slice(j * FF_CHUNK, (j + 1) * FF_CHUNK)
        u = jnp.maximum(_dot(hb, wup_ref[:, fs]), 0.0)
        f = f + _dot((u * u).astype(BF16), wdn_ref[fs, :])
    o_ref[...] = _layernorm(ALPHA * h + f, g2_ref[...], b2_ref[...])


def _out_call(x, sb, ssd, mla, w, tm):
    n = x.shape[0]
    row = lambda width: pl.BlockSpec((tm, width), lambda i: (i, 0))
    const = lambda shape: pl.BlockSpec(shape, lambda i: (0,) * len(shape), pipeline_mode=pl.Buffered(1))
    return pl.pallas_call(
        _out_body,
        grid=(n // tm,),
        in_specs=[row(D_MODEL), row(SB_W), row(SSD_INNER), row(MLA_W),
                  const((1, SB_W)), const((1, MLA_W)), const((D_MODEL, D_MODEL)),
                  const((1, D_MODEL)), const((1, D_MODEL)),
                  const((D_MODEL, D_FF)), const((D_FF, D_MODEL)),
                  const((1, D_MODEL)), const((1, D_MODEL))],
        out_specs=row(D_MODEL),
        out_shape=jax.ShapeDtypeStruct((n, D_MODEL), F32),
        compiler_params=_params(("arbitrary",)),
        name="out_ffn",
    )(x, sb, ssd, mla, w["sb_norm"], w["mla_norm"], w["w_out"], w["ln1_g"], w["ln1_b"],
      w["w_up"], w["w_down"], w["ln2_g"], w["ln2_b"])


def _rot_cols(wm):
    half = wm.shape[-1] // 2
    return jnp.concatenate([-wm[..., half:], wm[..., :half]], axis=-1)


def _layer_weights(l, w_in, sb_norm, conv_w, conv_b, dt_bias, a_log, d_skip, ssd_norm, q_norm, w_uq, kv_norm,
                   w_uk, w_uv, mla_norm, w_out, ln1_g, ln1_b, w_up, w_down, ln2_g, ln2_b):
    wi = w_in[l]
    o = [0, 256, 384, 512, 1024, 2048, 2056, 2312, 2568, 2600]
    sq, sk, sv, z, xbc, dtc, cq, ckv, kpe = [wi[:, o[i]:o[i + 1]] for i in range(9)]
    d = wi.shape[0]
    dt_pad = jnp.concatenate([dtc, jnp.zeros((d, LANE - SSD_HEADS), F32)], axis=1)
    kpe_grp = jnp.concatenate([kpe, _rot_cols(kpe), jnp.zeros((d, LANE - 2 * MLA_ROPE), F32)], axis=1)
    w1 = jnp.concatenate([sq, sk, sv, z, xbc, cq, ckv, dt_pad, kpe_grp], axis=1).astype(BF16)

    uq = w_uq[l]
    zq = lambda width: jnp.zeros((MLA_QR, MLA_HEADS, width), F32)
    pe = uq[..., MLA_NOPE:]
    w2a = jnp.concatenate([uq[..., :MLA_NOPE], pe, zq(HEAD_PAD - MLA_NOPE - MLA_ROPE)], axis=-1)
    w2b = jnp.concatenate([zq(MLA_NOPE), _rot_cols(pe), zq(HEAD_PAD - MLA_NOPE - MLA_ROPE)], axis=-1)
    w2 = jnp.concatenate([w2a.reshape(MLA_QR, -1), w2b.reshape(MLA_QR, -1)], axis=1).astype(BF16)

    uk = w_uk[l]
    wk_prompt = jnp.concatenate([uk, jnp.zeros((MLA_KVR, MLA_HEADS, HEAD_PAD - MLA_NOPE), F32)], axis=-1)
    wk_prompt = wk_prompt.reshape(MLA_KVR, MLA_HEADS * HEAD_PAD).astype(BF16)
    ukt = jnp.transpose(uk, (1, 2, 0))
    wk_decode = jnp.concatenate([ukt, jnp.zeros((MLA_HEADS, HEAD_PAD - MLA_NOPE, MLA_KVR), F32)], axis=1)
    wk_decode = wk_decode.reshape(MLA_HEADS * HEAD_PAD, MLA_KVR).astype(BF16)
    wv = w_uv[l].reshape(MLA_KVR, MLA_W).astype(BF16)

    pad_heads = lambda v: jnp.concatenate([v, jnp.zeros((LANE - SSD_HEADS,), F32)])[None, :]
    return {
        "w1": w1, "w2": w2, "wk_prompt": wk_prompt, "wk_decode": wk_decode, "wv": wv, "wv_t": wv.T,
        "q_norm": q_norm[l][None, :], "kv_norm": kv_norm[l][None, :],
        "conv_w": conv_w[l], "conv_b": conv_b[l][None, :],
        "dt_bias": pad_heads(dt_bias[l]), "a_log": pad_heads(a_log[l]),
        "d_skip_e": jnp.repeat(d_skip[l], SSD_P)[None, :], "ssd_norm": ssd_norm[l][None, :],
        "sb_norm": sb_norm[l][None, :], "mla_norm": mla_norm[l][None, :],
        "w_out": w_out[l].astype(BF16), "ln1_g": ln1_g[l][None, :], "ln1_b": ln1_b[l][None, :],
        "w_up": w_up[l].astype(BF16), "w_down": w_down[l].astype(BF16),
        "ln2_g": ln2_g[l][None, :], "ln2_b": ln2_b[l][None, :],
    }


def _rope_table(pos):
    half = MLA_ROPE // 2
    inv = ROPE_THETA ** (-jnp.arange(half, dtype=F32) / half)
    ang = pos.astype(F32)[:, None] * inv[None, :]
    cos, sin = jnp.cos(ang), jnp.sin(ang)
    cos32 = jnp.concatenate([cos, cos], axis=1)
    sin32 = jnp.concatenate([sin, sin], axis=1)
    t = pos.shape[0]
    ones, zeros = jnp.ones, jnp.zeros
    cosq = jnp.concatenate([ones((t, MLA_NOPE), F32), cos32, zeros((t, 32), F32)], axis=1)
    sinq = jnp.concatenate([zeros((t, MLA_NOPE), F32), sin32, zeros((t, 32), F32)], axis=1)
    cosk = jnp.concatenate([cos32, zeros((t, 96), F32)], axis=1)
    sink = jnp.concatenate([sin32, zeros((t, 96), F32)], axis=1)
    return jnp.concatenate([cosq, sinq, cosk, sink], axis=1)


def _pick(n, candidates):
    for c in candidates:
        if n % c == 0:
            return c
    return n


def kernel(x_prompt, x_sample, cache_sb_k, cache_sb_v, cache_mla_ckv, cache_mla_kpe, state_ssm, state_conv,
           page_table, w_in, sb_norm, conv_w, conv_b, dt_bias, a_log, d_skip, ssd_norm, q_norm, w_uq, kv_norm,
           w_uk, w_uv, mla_norm, w_out, ln1_g, ln1_b, w_up, w_down, ln2_g, ln2_b):
    bp, seq, d = x_prompt.shape
    nd, dec_seq, _ = x_sample.shape
    assert dec_seq == 1 and d == D_MODEL
    depth = w_in.shape[0]
    n_pages = page_table.shape[1]
    past_len = n_pages * PAGE
    n_pool = cache_sb_k.shape[1]

    tab_p = _rope_table(jnp.arange(seq, dtype=jnp.int32))
    tab_s = _rope_table(jnp.full((nd,), past_len, dtype=jnp.int32))
    kc = jnp.transpose(cache_sb_k, (0, 1, 3, 4, 2)).reshape(depth, n_pool, SB_KV * SB_DIM, PAGE)
    vc = jnp.transpose(cache_sb_v, (0, 1, 3, 4, 2)).reshape(depth, n_pool, SB_KV * SB_DIM, PAGE)
    pc = jnp.transpose(cache_mla_kpe, (0, 1, 3, 2))

    tm_p = _pick(seq, (256, 128))
    tq = _pick(seq, (256, 128))
    chunk = _pick(seq, (128,))
    tq_mla = _pick(seq, (2 * tm_p, tm_p))

    hp = x_prompt.reshape(bp * seq, d)
    hs = x_sample.reshape(nd, d)
    new_p, new_s = [], []
    for l in range(depth):
        w = _layer_weights(l, w_in, sb_norm, conv_w, conv_b, dt_bias, a_log, d_skip, ssd_norm, q_norm, w_uq,
                           kv_norm, w_uk, w_uv, mla_norm, w_out, ln1_g, ln1_b, w_up, w_down, ln2_g, ln2_b)
        wp = dict(w, wk=w["wk_prompt"], wv=w["wv_t"])
        (sq, sk, sv, z, xbc, dt, ckv, kpe, kb, vb, qh, kh, vmt) = _proj_call(hp, tab_p, wp, False, tm_p)
        sb_o = _sb_prompt_call(sq, kb, vb, bp, seq, tq)
        mla_o = _mla_prompt_call(qh, kh, vmt, bp, seq, tq_mla, tm_p)
        ssd_o, ssm_new, conv_new = _ssd_prompt_call(z, xbc, dt, w, bp, seq, chunk)
        hp = _out_call(hp, sb_o, ssd_o, mla_o, w, tm_p)
        new_p.append((sk.reshape(bp, seq, SB_KV, SB_DIM), sv.reshape(bp, seq, SB_KV, SB_DIM),
                      ckv.reshape(bp, seq, MLA_KVR), kpe.reshape(bp, seq, MLA_ROPE),
                      ssm_new.reshape(bp, SSD_HEADS, SSD_P, SSD_N), conv_new))
        wd = dict(w, wk=w["wk_decode"])
        (sq, sk, sv, z, xbc, dt, ckv, kpe, qh, qlat) = _proj_call(hs, tab_s, wd, True, nd)
        q4 = sq.reshape(nd, SB_HEADS, SB_DIM)
        zq = jnp.zeros_like(q4)
        grp = (jnp.arange(SB_HEADS) // (SB_HEADS // SB_KV))[None, :, None]
        qbd = jnp.concatenate([jnp.where(grp == 0, q4, zq), jnp.where(grp == 1, q4, zq)], axis=-1)
        qbd = jnp.concatenate([qbd, jnp.zeros_like(qbd)], axis=1)
        sb_o = _sb_decode_call(page_table, qbd, kc, vc, l).reshape(nd, SB_W)
        pad8 = lambda a: jnp.concatenate([a, jnp.zeros_like(a)], axis=1)
        qlat8 = pad8(qlat.reshape(nd, MLA_HEADS, MLA_KVR))
        qpe8 = pad8(qh.reshape(nd, MLA_HEADS, HEAD_PAD)[:, :, MLA_NOPE:MLA_NOPE + MLA_ROPE])
        mla_o = _mla_decode_call(page_table, qlat8, qpe8, ckv[:, None, :], kpe[:, None, :], w["wv"],
                                 cache_mla_ckv, pc, l).reshape(nd, MLA_W)
        conv_t = jnp.transpose(state_conv[l], (1, 0, 2))
        h0 = state_ssm[l].reshape(nd, SSD_INNER, SSD_N)
        ssd_o, ssm_new, conv_new_t = _ssd_decode_call(z, xbc, dt, conv_t, h0, w)
        hs = _out_call(hs, sb_o, ssd_o, mla_o, w, nd)
        new_s.append((sk.reshape(nd, 1, SB_KV, SB_DIM), sv.reshape(nd, 1, SB_KV, SB_DIM),
                      ckv.reshape(nd, 1, MLA_KVR), kpe.reshape(nd, 1, MLA_ROPE),
                      ssm_new.reshape(nd, SSD_HEADS, SSD_P, SSD_N), jnp.transpose(conv_new_t, (1, 0, 2))))

    stack = lambda states, i: jnp.stack([st[i] for st in states])
    return (hp.reshape(bp, seq, d), hs.reshape(nd, 1, d),
            *[stack(new_p, i) for i in range(6)], *[stack(new_s, i) for i in range(6)])
```

```python
import functools
import math

import jax
import jax.numpy as jnp
from jax import lax
from jax.experimental import pallas as pl
from jax.experimental.pallas import tpu as pltpu

F32 = jnp.float32
BF16 = jnp.bfloat16

D_MODEL = 1024
PAGE = 128
SB_HEADS = 4
SB_KV = 2
SB_DIM = 64
SB_W = SB_HEADS * SB_DIM
SSD_HEADS = 8
SSD_P = 64
SSD_INNER = SSD_HEADS * SSD_P
SSD_GROUPS = 2
SSD_N = 128
SSD_CONV = 4
SSD_CH = SSD_INNER + 2 * SSD_GROUPS * SSD_N
MLA_HEADS = 4
MLA_NOPE = 64
MLA_ROPE = 32
MLA_V = 64
MLA_QR = 256
MLA_KVR = 256
MLA_W = MLA_HEADS * MLA_V
MLA_SCALE = (MLA_NOPE + MLA_ROPE) ** -0.5
LOG2_E = math.log2(math.e)
ROPE_THETA = 10000.0
D_FF = 4 * D_MODEL
EPS = 1e-5
DEPTH = 2
ALPHA = (2 * DEPTH) ** 0.25
LANE = 128
HEAD_PAD = 128

SB_DEAD = -104.0

VMEM_LIMIT = 56 * 1024 * 1024

C_SQ, C_SK, C_SV, C_Z, C_XBC, C_CQ, C_CKV, C_DT, C_KPE, C_END = (
    0, 256, 384, 512, 1024, 2048, 2304, 2560, 2688, 2816)


def _dot(a, b):
    return jnp.dot(a, b, preferred_element_type=F32)


def _dot_nt(a, b):
    return lax.dot_general(a, b, (((1,), (1,)), ((), ())), preferred_element_type=F32)


def _dot_tn(a, b):
    return lax.dot_general(a, b, (((0,), (0,)), ((), ())), preferred_element_type=F32)


def _split_hi_lo(x):
    hi = x.astype(BF16)
    lo = (x - hi.astype(F32)).astype(BF16)
    return hi, lo


def _dot_f32ish(x, w01):
    hi, lo = _split_hi_lo(x)
    return _dot(hi, w01) + _dot(lo, w01)


def _rms(x, g):
    return x * lax.rsqrt(jnp.mean(x * x, axis=-1, keepdims=True) + EPS) * g


def _layernorm(x, g, b):
    mu = jnp.mean(x, axis=-1, keepdims=True)
    xc = x - mu
    var = jnp.mean(xc * xc, axis=-1, keepdims=True)
    return xc * lax.rsqrt(var + EPS) * g + b


def _softplus(x):
    return jnp.maximum(x, 0.0) + jnp.log1p(jnp.exp(-jnp.abs(x)))


def _silu(x):
    return x / (1.0 + jnp.exp(-x))


def _full(shape):
    nd = len(shape)
    return pl.BlockSpec(shape, lambda *_: (0,) * nd)


def _params(sem):
    return pltpu.CompilerParams(dimension_semantics=sem, vmem_limit_bytes=VMEM_LIMIT)


def _proj_body(decode, x_ref, tab_ref, w1_ref, w2_ref, wk_ref, wv_ref, qg_ref, kvg_ref, *outs):
    if decode:
        (sq_ref, sk_ref, sv_ref, z_ref, xbc_ref, dt_ref, ckv_ref, kpe_ref, qh_ref, qlat_ref) = outs
    else:
        (sq_ref, sk_ref, sv_ref, z_ref, xbc_ref, dt_ref, ckv_ref, kpe_ref,
         kb_ref, vb_ref, qh_ref, kh_ref, vm_ref) = outs
    x = x_ref[...].astype(BF16)

    def col(lo, hi):
        return _dot(x, w1_ref[:, lo:hi])

    sq_ref[...] = col(C_SQ, C_SK)
    sk = col(C_SK, C_SV)
    sv = col(C_SV, C_Z)
    sk_ref[...] = sk
    sv_ref[...] = sv
    z_ref[...] = col(C_Z, C_XBC)
    xbc_ref[...] = col(C_XBC, C_CQ)
    dt_ref[...] = col(C_DT, C_KPE)

    tab = tab_ref[...]
    lane = lax.broadcasted_iota(jnp.int32, tab.shape, 1)
    rope_q = jnp.logical_and(lane >= MLA_NOPE, lane < MLA_NOPE + MLA_ROPE)
    cosq = jnp.where(lane < MLA_NOPE, 1.0, jnp.where(rope_q, pltpu.roll(tab, 64, 1), 0.0))
    sinq = jnp.where(rope_q, pltpu.roll(tab, 32, 1), 0.0)
    cosk = jnp.where(lane < MLA_ROPE, tab, 0.0)
    sink = jnp.where(lane < MLA_ROPE, pltpu.roll(tab, 96, 1), 0.0)

    cqn = _rms(col(C_CQ, C_CKV), qg_ref[...]).astype(BF16)
    qa = _dot(cqn, w2_ref[:, 0:512])
    qb = _dot(cqn, w2_ref[:, 512:1024])
    q_scale = MLA_SCALE if decode else MLA_SCALE * LOG2_E
    qh = []
    for h in range(MLA_HEADS):
        sl = slice(h * HEAD_PAD, (h + 1) * HEAD_PAD)
        qh.append((qa[:, sl] * cosq + qb[:, sl] * sinq) * q_scale)

    ckvn = _rms(col(C_CKV, C_DT), kvg_ref[...])
    ckv_ref[...] = ckvn
    kg = col(C_KPE, C_END)
    kpe = kg * cosk + pltpu.roll(kg, 96, 1) * sink
    kpe_ref[...] = kpe[:, 0:MLA_ROPE]

    if decode:
        for h in range(MLA_HEADS):
            sl = slice(h * HEAD_PAD, (h + 1) * HEAD_PAD)
            qh_ref[:, sl] = qh[h]
            qlat_ref[:, h * MLA_KVR:(h + 1) * MLA_KVR] = _dot(qh[h].astype(BF16), wk_ref[sl, :])
    else:
        kb_ref[...] = sk.astype(BF16)
        vb_ref[...] = sv.astype(BF16)
        ckvb = ckvn.astype(BF16)
        kpe_at64 = pltpu.roll(kpe, 64, 1)
        for h in range(MLA_HEADS):
            sl = slice(h * HEAD_PAD, (h + 1) * HEAD_PAD)
            qh_ref[:, sl] = qh[h].astype(BF16)
            kh_ref[:, sl] = (_dot(ckvb, wk_ref[:, sl]) + kpe_at64).astype(BF16)
        vm_ref[0] = _dot_nt(wv_ref[...], ckvb).astype(BF16)


def _proj_call(x, tab, w, decode, tm):
    n = x.shape[0]
    nt = tab.shape[0] // tm
    grid = (n // tm,)
    row = lambda width: pl.BlockSpec((tm, width), lambda i: (i, 0))
    in_specs = [
        row(D_MODEL),
        pl.BlockSpec((tm, LANE), lambda i: (i % nt, 0)),
        _full(w["w1"].shape), _full(w["w2"].shape), _full(w["wk"].shape), _full(w["wv"].shape),
        _full((1, MLA_QR)), _full((1, MLA_KVR)),
    ]
    f = lambda width, dt=F32: jax.ShapeDtypeStruct((n, width), dt)
    out_shape = [f(256), f(128), f(128), f(512), f(1024), f(128), f(256), f(MLA_ROPE)]
    out_specs = [row(256), row(128), row(128), row(512), row(1024), row(128), row(256), row(MLA_ROPE)]
    if decode:
        out_shape += [f(512), f(MLA_HEADS * MLA_KVR)]
        out_specs += [row(512), row(MLA_HEADS * MLA_KVR)]
    else:
        out_shape += [f(128, BF16), f(128, BF16), f(512, BF16), f(512, BF16),
                      jax.ShapeDtypeStruct((n // tm, MLA_W, tm), BF16)]
        out_specs += [row(128), row(128), row(512), row(512),
                      pl.BlockSpec((1, MLA_W, tm), lambda i: (i, 0, 0))]
    return pl.pallas_call(
        functools.partial(_proj_body, decode),
        grid=grid, in_specs=in_specs, out_specs=out_specs, out_shape=out_shape,
        compiler_params=_params(("arbitrary",)),
        name="proj_decode" if decode else "proj_prompt",
    )(x, tab, w["w1"], w["w2"], w["wk"], w["wv"], w["q_norm"], w["kv_norm"])


def _suffix_matrix(tk):
    j = lax.broadcasted_iota(jnp.int32, (tk, tk), 0)
    s = lax.broadcasted_iota(jnp.int32, (tk, tk), 1)
    return jnp.where(j > s, 1.0, 0.0).astype(BF16)


def _sb_block(qall, kblk, vblk, umat, c, mask, keys_on_lanes=False):
    z = (_dot(qall, kblk) if keys_on_lanes else _dot_nt(qall, kblk)) * (SB_DIM ** -0.5)
    lse = jnp.log1p(jnp.exp(-jnp.abs(z)))
    log_beta = jnp.minimum(z, 0.0) - lse
    log_keep = log_beta - z
    if mask is not None:
        log_keep = jnp.where(mask, log_keep, 0.0)
    hi, lo = _split_hi_lo(log_keep)
    after = _dot(hi, umat) + _dot(lo, umat)
    w = jnp.exp(log_beta + after + c)
    if mask is not None:
        w = jnp.where(mask, w, 0.0)
    wb = w.astype(BF16)
    pv = _dot_nt(wb, vblk) if keys_on_lanes else _dot(wb, vblk)
    return pv, c + jnp.sum(log_keep, axis=1, keepdims=True)


def _sb_prompt_body(q_ref, k_ref, v_ref, o_ref, qall_ref, acc_ref, c_ref, *, tq):
    i = pl.program_id(1)
    tk = tq
    lane = lax.broadcasted_iota(jnp.int32, (tq, LANE), 1)
    lo_half = lane < SB_DIM
    s0 = q_ref[:, 0:128]
    s1 = q_ref[:, 128:256]
    qall_ref[0 * tq:1 * tq, :] = jnp.where(lo_half, s0, 0.0).astype(BF16)
    qall_ref[1 * tq:2 * tq, :] = jnp.where(lo_half, pltpu.roll(s0, 64, 1), 0.0).astype(BF16)
    qall_ref[2 * tq:3 * tq, :] = jnp.where(lo_half, 0.0, pltpu.roll(s1, 64, 1)).astype(BF16)
    qall_ref[3 * tq:4 * tq, :] = jnp.where(lo_half, 0.0, s1).astype(BF16)
    umat = _suffix_matrix(tk)

    row = lax.rem(lax.broadcasted_iota(jnp.int32, (SB_HEADS * tq, tk), 0), tq)
    colm = lax.broadcasted_iota(jnp.int32, (SB_HEADS * tq, tk), 1)
    mask = colm < row

    def run(kb, c, m):
        start = pl.multiple_of(kb * tk, tk)
        return _sb_block(qall_ref[...], k_ref[pl.ds(start, tk), :], v_ref[pl.ds(start, tk), :], umat, c, m)

    pv, c = run(i, jnp.zeros((SB_HEADS * tq, 1), F32), mask)
    acc_ref[...] = pv
    c_ref[...] = c

    def cond(st):
        kb, cmax = st
        return jnp.logical_and(kb >= 0, cmax > SB_DEAD)

    def body(st):
        kb, _ = st
        pv, c = run(kb, c_ref[...], None)
        acc_ref[...] += pv
        c_ref[...] = c
        return kb - 1, jnp.max(c)

    lax.while_loop(cond, body, (i - 1, jnp.max(c)))

    o0 = acc_ref[0 * tq:1 * tq, :]
    o1 = acc_ref[1 * tq:2 * tq, :]
    o2 = acc_ref[2 * tq:3 * tq, :]
    o3 = acc_ref[3 * tq:4 * tq, :]
    o_ref[:, 0:128] = jnp.where(lo_half, o0, pltpu.roll(o1, 64, 1))
    o_ref[:, 128:256] = jnp.where(lo_half, pltpu.roll(o2, 64, 1), o3)


def _sb_prompt_call(sq, kb, vb, bsz, seq, tq):
    nq = seq // tq
    return pl.pallas_call(
        functools.partial(_sb_prompt_body, tq=tq),
        grid=(bsz, nq),
        in_specs=[
            pl.BlockSpec((tq, SB_W), lambda b, i: (b * nq + i, 0)),
            pl.BlockSpec((seq, 128), lambda b, i: (b, 0)),
            pl.BlockSpec((seq, 128), lambda b, i: (b, 0)),
        ],
        out_specs=pl.BlockSpec((tq, SB_W), lambda b, i: (b * nq + i, 0)),
        out_shape=jax.ShapeDtypeStruct((bsz * seq, SB_W), F32),
        scratch_shapes=[
            pltpu.VMEM((SB_HEADS * tq, LANE), BF16),
            pltpu.VMEM((SB_HEADS * tq, LANE), F32),
            pltpu.VMEM((SB_HEADS * tq, 1), F32),
        ],
        compiler_params=_params(("arbitrary", "arbitrary")),
        name="sb_prompt",
    )(sq, kb, vb)


SB_DEC_PAGES = 2


def _sb_decode_body(pt_ref, q_ref, kc_ref, vc_ref, o_ref, kbuf, vbuf, sem, *, layer, n_pages):
    s = pl.program_id(0)
    g_pages = SB_DEC_PAGES
    ng = n_pages // g_pages
    tk = g_pages * PAGE

    def copies(seq, g, slot):
        out = []
        for j in range(g_pages):
            page = pt_ref[seq, g * g_pages + j]
            dst = pl.ds(j * PAGE, PAGE)
            out.append(pltpu.make_async_copy(kc_ref.at[layer, page], kbuf.at[slot, :, dst], sem.at[0, slot]))
            out.append(pltpu.make_async_copy(vc_ref.at[layer, page], vbuf.at[slot, :, dst], sem.at[1, slot]))
        return out

    def start(seq, g, slot):
        for cp in copies(seq, g, slot):
            cp.start()

    def wait(seq, g, slot):
        for cp in copies(seq, g, slot):
            cp.wait()

    newest_slot = lambda seq: 2 + lax.rem(seq, 2)
    slot_of = lambda g: lax.rem(ng - 2 - g, 2)
    umat = _suffix_matrix(tk)
    qall = q_ref[0].astype(BF16)

    def step(slot, c, acc):
        pv, c = _sb_block(qall, kbuf[slot].astype(BF16), vbuf[slot].astype(BF16), umat, c, None,
                          keys_on_lanes=True)
        return jnp.max(c[0:SB_HEADS]), c, acc + pv

    @pl.when(s == 0)
    def _():
        start(s, ng - 1, newest_slot(s))

    @pl.when(s + 1 < pl.num_programs(0))
    def _():
        start(s + 1, ng - 1, newest_slot(s + 1))

    if ng >= 2:
        start(s, ng - 2, 0)
    wait(s, ng - 1, newest_slot(s))
    first = step(newest_slot(s), jnp.zeros((8, 1), F32), jnp.zeros((8, LANE), F32))

    def cond(st):
        g, cmax, _, _ = st
        return jnp.logical_and(g >= 0, cmax > SB_DEAD)

    def body(st):
        g, _, c, acc = st
        slot = slot_of(g)

        @pl.when(g >= 1)
        def _():
            start(s, g - 1, 1 - slot)

        wait(s, g, slot)
        return (g - 1, *step(slot, c, acc))

    g_end, _, _, acc = lax.while_loop(cond, body, (jnp.int32(ng - 2), *first))

    @pl.when(g_end >= 0)
    def _():
        wait(s, g_end, slot_of(g_end))

    lane = lax.broadcasted_iota(jnp.int32, (1, LANE), 1)
    lo_half = lane < SB_DIM
    o_ref[0, :, 0:128] = jnp.where(lo_half, acc[0:1], pltpu.roll(acc[1:2], 64, 1))
    o_ref[0, :, 128:256] = jnp.where(lo_half, pltpu.roll(acc[2:3], 64, 1), acc[3:4])


def _sb_decode_call(page_table, qbd, kc, vc, layer):
    n, n_pages = page_table.shape
    grid_spec = pltpu.PrefetchScalarGridSpec(
        num_scalar_prefetch=1,
        grid=(n,),
        in_specs=[
            pl.BlockSpec((1, 8, LANE), lambda s, pt: (s, 0, 0)),
            pl.BlockSpec(memory_space=pl.ANY),
            pl.BlockSpec(memory_space=pl.ANY),
        ],
        out_specs=pl.BlockSpec((1, 1, SB_W), lambda s, pt: (s, 0, 0)),
        scratch_shapes=[
            pltpu.VMEM((4, LANE, SB_DEC_PAGES * PAGE), F32),
            pltpu.VMEM((4, LANE, SB_DEC_PAGES * PAGE), F32),
            pltpu.SemaphoreType.DMA((2, 4)),
        ],
    )
    return pl.pallas_call(
        functools.partial(_sb_decode_body, layer=layer, n_pages=n_pages),
        grid_spec=grid_spec,
        out_shape=jax.ShapeDtypeStruct((n, 1, SB_W), F32),
        compiler_params=_params(("arbitrary",)),
        name="sb_decode",
    )(page_table, qbd, kc, vc)


def _mla_prompt_body(q_ref, k_ref, vt_ref, o_ref, *, tq, tk):
    i = pl.program_id(1)
    ratio = tq // tk
    key = lax.broadcasted_iota(jnp.int32, (tk, tq), 0)
    qry = lax.broadcasted_iota(jnp.int32, (tk, tq), 1)

    heads = range(MLA_HEADS)

    def scores(kb):
        start = pl.multiple_of(kb * tk, tk)
        return tuple(_dot_nt(k_ref[pl.ds(start, tk), h * HEAD_PAD:(h + 1) * HEAD_PAD],
                             q_ref[:, h * HEAD_PAD:(h + 1) * HEAD_PAD]) for h in heads)

    def absorb(kb, sts, state, mask):
        ps, ms, ls, alphas = [], [], [], []
        for h in heads:
            m_old, l_old, _ = state[h]
            st = sts[h] if mask is None else jnp.where(mask, sts[h], -jnp.inf)
            m_new = jnp.maximum(m_old, jnp.max(st, axis=0, keepdims=True))
            p = jnp.exp2(st - m_new)
            alpha = jnp.exp2(m_old - m_new)
            ls.append(alpha * l_old + jnp.sum(p, axis=0, keepdims=True))
            ms.append(m_new)
            alphas.append(alpha)
            ps.append(p.astype(BF16))
        pvs = [_dot(vt_ref[kb, h * MLA_V:(h + 1) * MLA_V, :], ps[h]) for h in heads]
        return tuple((ms[h], ls[h], alphas[h] * state[h][2] + pvs[h]) for h in heads)

    init = tuple((jnp.full((1, tq), -jnp.inf, F32), jnp.zeros((1, tq), F32), jnp.zeros((MLA_V, tq), F32))
                 for _ in range(MLA_HEADS))
    n_full = i * ratio

    state = lax.fori_loop(0, n_full, lambda kb, st: absorb(kb, scores(kb), st, None), init)
    for j in range(ratio):
        state = absorb(n_full + j, scores(n_full + j), state, key + j * tk <= qry)
    o_t = jnp.concatenate([acc / l for (_, l, acc) in state], axis=0)
    o_ref[...] = o_t.T


def _mla_prompt_call(qh, kh, vmt, bsz, seq, tq, tk):
    nq = seq // tq
    nk = seq // tk
    return pl.pallas_call(
        functools.partial(_mla_prompt_body, tq=tq, tk=tk),
        grid=(bsz, nq),
        in_specs=[
            pl.BlockSpec((tq, 512), lambda b, i: (b * nq + i, 0)),
            pl.BlockSpec((seq, 512), lambda b, i: (b, 0)),
            pl.BlockSpec((nk, MLA_W, tk), lambda b, i: (b, 0, 0)),
        ],
        out_specs=pl.BlockSpec((tq, MLA_W), lambda b, i: (b * nq + i, 0)),
        out_shape=jax.ShapeDtypeStruct((bsz * seq, MLA_W), F32),
        compiler_params=_params(("arbitrary", "arbitrary")),
        name="mla_prompt",
    )(qh, kh, vmt)


MLA_DEC_PAGES = 16
MLA_DEC_BUFS = 4


def _mla_decode_body(pt_ref, ql_ref, qp_ref, cn_ref, pn_ref, wuv_ref, cc_ref, pc_ref, o_ref,
                     cbuf, pbuf, cb16, sc_keep, sem, *, layer, n_pages):
    s = pl.program_id(0)
    n = pl.num_programs(0)
    g_pages = MLA_DEC_PAGES
    ng = n_pages // g_pages
    nbuf = MLA_DEC_BUFS
    total = n * ng

    def copies(t):
        seq_raw = lax.div(t, jnp.int32(ng))
        g = t - seq_raw * ng
        seq = lax.rem(seq_raw, n)
        slot = lax.rem(t, jnp.int32(nbuf))
        out = []
        for j in range(g_pages):
            page = pt_ref[seq, g * g_pages + j]
            dst = pl.ds(j * PAGE, PAGE)
            out.append(pltpu.make_async_copy(cc_ref.at[layer, page], cbuf.at[slot, dst], sem.at[0, slot]))
            out.append(pltpu.make_async_copy(pc_ref.at[layer, page], pbuf.at[slot, :, dst], sem.at[1, slot]))
        return out

    def start(t):
        for cp in copies(t):
            cp.start()

    def wait(t):
        for cp in copies(t):
            cp.wait()

    def scores(t, seq):
        wait(t)
        slot = lax.rem(t, jnp.int32(nbuf))
        cb = cbuf[slot].astype(BF16)
        cb16[lax.rem(t, jnp.int32(2))] = cb
        return (_dot_nt(ql_ref[seq].astype(BF16), cb)
                + _dot(qp_ref[seq].astype(BF16), pbuf[slot].astype(BF16)))

    def refill(t):
        @pl.when(t + nbuf <= total)
        def _():
            start(t + nbuf)

    def absorb(t, sc, st):
        m_old, l_old, acc = st
        m_new = jnp.maximum(m_old, jnp.max(sc, axis=1, keepdims=True))
        p = jnp.exp(sc - m_new)
        alpha = jnp.exp(m_old - m_new)
        pv = _dot(p.astype(BF16), cb16[lax.rem(t, jnp.int32(2))])
        return m_new, alpha * l_old + jnp.sum(p, axis=1, keepdims=True), alpha * acc + pv

    @pl.when(s == 0)
    def _():
        for t0 in range(nbuf):
            start(t0)
        sc_keep[...] = scores(0, 0)

    t_first = s * ng

    def body(g, carry):
        sc, st = carry
        t = t_first + g
        sc_next = scores(t + 1, s)
        st = absorb(t, sc, st)
        refill(t)
        return sc_next, st

    init = (jnp.full((8, 1), -jnp.inf, F32), jnp.zeros((8, 1), F32), jnp.zeros((8, MLA_KVR), F32))
    sc, st = lax.fori_loop(0, ng - 1, body, (sc_keep[...], init))
    t_last = t_first + ng - 1
    sc_keep[...] = scores(t_last + 1, lax.rem(s + 1, n))
    m_old, l_old, acc = absorb(t_last, sc, st)
    refill(t_last)
    ql = ql_ref[s]
    qp = qp_ref[s]

    cn = cn_ref[0]
    pn = pn_ref[0]
    sc = jnp.sum(ql * cn, axis=1, keepdims=True) + jnp.sum(qp * pn, axis=1, keepdims=True)
    m_new = jnp.maximum(m_old, sc)
    p = jnp.exp(sc - m_new)
    alpha = jnp.exp(m_old - m_new)
    l_new = alpha * l_old + p
    o_lat = (alpha * acc + p * cn) / l_new
    o_all = _dot(o_lat.astype(BF16), wuv_ref[...])
    sub = lax.broadcasted_iota(jnp.int32, (8, MLA_W), 0)
    lane = lax.broadcasted_iota(jnp.int32, (8, MLA_W), 1)
    pick = (lane // MLA_V) == sub
    o_ref[0] = jnp.sum(jnp.where(pick, o_all, 0.0), axis=0, keepdims=True)


def _mla_decode_call(page_table, qlat, qpe, ckv_new, kpe_new, wuv, cc, pc, layer):
    n, n_pages = page_table.shape
    tk = MLA_DEC_PAGES * PAGE
    grid_spec = pltpu.PrefetchScalarGridSpec(
        num_scalar_prefetch=1,
        grid=(n,),
        in_specs=[
            pl.BlockSpec((n, 8, MLA_KVR), lambda s, pt: (0, 0, 0)),
            pl.BlockSpec((n, 8, MLA_ROPE), lambda s, pt: (0, 0, 0)),
            pl.BlockSpec((1, 1, MLA_KVR), lambda s, pt: (s, 0, 0)),
            pl.BlockSpec((1, 1, MLA_ROPE), lambda s, pt: (s, 0, 0)),
            pl.BlockSpec((MLA_KVR, MLA_W), lambda s, pt: (0, 0)),
            pl.BlockSpec(memory_space=pl.ANY),
            pl.BlockSpec(memory_space=pl.ANY),
        ],
        out_specs=pl.BlockSpec((1, 1, MLA_W), lambda s, pt: (s, 0, 0)),
        scratch_shapes=[
            pltpu.VMEM((MLA_DEC_BUFS, tk, MLA_KVR), F32),
            pltpu.VMEM((MLA_DEC_BUFS, MLA_ROPE, tk), F32),
            pltpu.VMEM((2, tk, MLA_KVR), BF16),
            pltpu.VMEM((8, tk), F32),
            pltpu.SemaphoreType.DMA((2, MLA_DEC_BUFS)),
        ],
    )
    assert n_pages % MLA_DEC_PAGES == 0 and n * (n_pages // MLA_DEC_PAGES) >= MLA_DEC_BUFS
    return pl.pallas_call(
        functools.partial(_mla_decode_body, layer=layer, n_pages=n_pages),
        grid_spec=grid_spec,
        out_shape=jax.ShapeDtypeStruct((n, 1, MLA_W), F32),
        compiler_params=_params(("arbitrary",)),
        name="mla_decode",
    )(page_table, qlat, qpe, ckv_new, kpe_new, wuv, cc, pc)


def _expand_matrix():
    r = lax.broadcasted_iota(jnp.int32, (LANE, SSD_INNER), 0)
    c = lax.broadcasted_iota(jnp.int32, (LANE, SSD_INNER), 1)
    return jnp.where(c // SSD_P == r, 1.0, 0.0).astype(BF16)


def _gated_norm(y, z, norm_w):
    g = y * _silu(z)
    half = SSD_INNER // SSD_GROUPS
    parts = []
    for k in range(SSD_GROUPS):
        gk = g[:, k * half:(k + 1) * half]
        parts.append(gk * lax.rsqrt(jnp.mean(gk * gk, axis=-1, keepdims=True) + EPS))
    return jnp.concatenate(parts, axis=1) * norm_w


def _ssd_prompt_body(z_ref, xbc_ref, dt_ref, cw_ref, cb_ref, dtb_ref, alog_ref, dskip_ref, nw_ref,
                     y_ref, hout_ref, cout_ref, xwin, ht, *, chunk):
    c = pl.program_id(1)
    nc = pl.num_programs(1)
    L = chunk

    @pl.when(c == 0)
    def _():
        xwin[0:8, :] = jnp.zeros((8, SSD_CH), F32)
        ht[...] = jnp.zeros(ht.shape, F32)

    xwin[8:8 + L, :] = xbc_ref[...]
    conv = cb_ref[...] + xwin[5:5 + L, :] * cw_ref[0:1, :]
    for i in range(1, SSD_CONV):
        conv = conv + xwin[5 + i:5 + i + L, :] * cw_ref[i:i + 1, :]
    xwin[0:8, :] = xwin[L:L + 8, :]
    cv = _silu(conv)
    xs = cv[:, 0:SSD_INNER]
    bm = cv[:, SSD_INNER:SSD_INNER + 256]
    cm = cv[:, SSD_INNER + 256:SSD_INNER + 512]

    dt = _softplus(dt_ref[...] + dtb_ref[...])
    da = dt * (-jnp.exp(alog_ref[...]))
    r = lax.broadcasted_iota(jnp.int32, (L, L), 0)
    s = lax.broadcasted_iota(jnp.int32, (L, L), 1)
    causal = s <= r
    tri = jnp.where(causal, 1.0, 0.0).astype(BF16)
    cs = _dot_f32ish_left(tri, da)
    cs_t = cs.T
    cs_last = cs[L - 1:L, :]
    ecs = jnp.exp(cs)
    emat = _expand_matrix()
    dt_e = _dot_f32ish(dt, emat)
    ecs_e = _dot_f32ish(ecs, emat)
    te_e = _dot_f32ish(dt * jnp.exp(cs_last - cs), emat)
    last_e = _dot_f32ish(jnp.exp(cs_last), emat)

    x_dt = (xs * dt_e).astype(BF16)
    x_end = (xs * te_e).astype(BF16)
    lane = lax.broadcasted_iota(jnp.int32, (L, LANE), 1)
    lo_half = lane < SSD_P
    y_parts = []
    for g in range(SSD_GROUPS):
        gs = slice(g * SSD_N, (g + 1) * SSD_N)
        cg = cm[:, gs].astype(BF16)
        bg = bm[:, gs].astype(BF16)
        cb = _dot_nt(cg, bg)
        hs = slice(g * 256, (g + 1) * 256)
        h_in = ht[g]
        y_off = _dot(cg, h_in.astype(BF16)) * ecs_e[:, hs]
        yd = []
        for hh in range(4):
            h = g * 4 + hh
            seg = cs[:, h:h + 1] - cs_t[h:h + 1, :]
            decay = jnp.exp(jnp.where(causal, seg, -jnp.inf))
            sc = (cb * decay).astype(BF16)
            ps = slice((h // 2) * LANE, (h // 2 + 1) * LANE)
            yd.append(_dot(sc, x_dt[:, ps]))
        y_diag = jnp.concatenate([jnp.where(lo_half, yd[0], yd[1]), jnp.where(lo_half, yd[2], yd[3])], axis=1)
        y_parts.append(y_diag + y_off)
        ht[g] = h_in * last_e[:, hs] + _dot_tn(bg, x_end[:, hs])
    y = jnp.concatenate(y_parts, axis=1) + dskip_ref[...] * xs
    y_ref[...] = _gated_norm(y, z_ref[...], nw_ref[...])

    @pl.when(c == nc - 1)
    def _():
        for g in range(SSD_GROUPS):
            hout_ref[0, g * 256:(g + 1) * 256, :] = ht[g].T
        cout_ref[0] = xwin[0:8, :][5:8, :]


def _dot_f32ish_left(w01, x):
    hi, lo = _split_hi_lo(x)
    return _dot(w01, hi) + _dot(w01, lo)


def _ssd_prompt_call(z, xbc, dt, w, bsz, seq, chunk):
    nc = seq // chunk
    row = lambda width: pl.BlockSpec((chunk, width), lambda b, c: (b * nc + c, 0))
    return pl.pallas_call(
        functools.partial(_ssd_prompt_body, chunk=chunk),
        grid=(bsz, nc),
        in_specs=[row(SSD_INNER), row(SSD_CH), row(LANE),
                  _full((SSD_CONV, SSD_CH)), _full((1, SSD_CH)), _full((1, LANE)), _full((1, LANE)),
                  _full((1, SSD_INNER)), _full((1, SSD_INNER))],
        out_specs=[row(SSD_INNER),
                   pl.BlockSpec((1, SSD_INNER, SSD_N), lambda b, c: (b, 0, 0)),
                   pl.BlockSpec((1, SSD_CONV - 1, SSD_CH), lambda b, c: (b, 0, 0))],
        out_shape=[jax.ShapeDtypeStruct((bsz * seq, SSD_INNER), F32),
                   jax.ShapeDtypeStruct((bsz, SSD_INNER, SSD_N), F32),
                   jax.ShapeDtypeStruct((bsz, SSD_CONV - 1, SSD_CH), F32)],
        scratch_shapes=[pltpu.VMEM((chunk + 8, SSD_CH), F32), pltpu.VMEM((SSD_GROUPS, SSD_N, 256), F32)],
        compiler_params=_params(("arbitrary", "arbitrary")),
        name="ssd_prompt",
    )(z, xbc, dt, w["conv_w"], w["conv_b"], w["dt_bias"], w["a_log"], w["d_skip_e"], w["ssd_norm"])


def _ssd_decode_body(z_ref, xbc_ref, dt_ref, cbuf_ref, h0_ref, cw_ref, cb_ref, dtb_ref, alog_ref, dskip_ref,
                     nw_ref, y_ref, hout_ref, cout_ref, xt, dect, bsel, csel, yt, xs_keep):
    s = pl.program_id(0)
    n = pl.num_programs(0)

    @pl.when(s == 0)
    def _():
        xnew = xbc_ref[...]
        conv = cb_ref[...] + xnew * cw_ref[3:4, :]
        for i in range(SSD_CONV - 1):
            conv = conv + cbuf_ref[i] * cw_ref[i:i + 1, :]
        cout_ref[0] = cbuf_ref[1]
        cout_ref[1] = cbuf_ref[2]
        cout_ref[2] = xnew
        cv = _silu(conv)
        xs = cv[:, 0:SSD_INNER]
        xs_keep[...] = xs
        bsel[...] = cv[:, SSD_INNER:SSD_INNER + 256]
        csel[...] = cv[:, SSD_INNER + 256:SSD_INNER + 512]
        dt = _softplus(dt_ref[...] + dtb_ref[...])
        da = dt * (-jnp.exp(alog_ref[...]))
        emat = _expand_matrix()
        xt[...] = (xs * _dot_f32ish(dt, emat)).T
        dect[...] = _dot_f32ish(jnp.exp(da), emat).T
        yt[...] = jnp.zeros(yt.shape, F32)

    lane = lax.broadcasted_iota(jnp.int32, xt.shape, 1)
    here = lane == s
    x_col = jnp.sum(jnp.where(here, xt[...], 0.0), axis=1, keepdims=True)
    d_col = jnp.sum(jnp.where(here, dect[...], 0.0), axis=1, keepdims=True)
    b_row = bsel[pl.ds(s, 1), :]
    c_row = csel[pl.ds(s, 1), :]
    ys = []
    for g in range(SSD_GROUPS):
        hs = slice(g * 256, (g + 1) * 256)
        gs = slice(g * SSD_N, (g + 1) * SSD_N)
        h_new = h0_ref[0, hs, :] * d_col[hs] + x_col[hs] * b_row[:, gs]
        hout_ref[0, hs, :] = h_new
        ys.append(jnp.sum(h_new * c_row[:, gs], axis=1, keepdims=True))
    y_col = jnp.concatenate(ys, axis=0)
    yt[...] = jnp.where(here, y_col, yt[...])

    @pl.when(s == n - 1)
    def _():
        xs = xs_keep[...]
        y = yt[...].T + dskip_ref[...] * xs
        y_ref[...] = _gated_norm(y, z_ref[...], nw_ref[...])


def _ssd_decode_call(z, xbc, dt, conv_t, h0, w, layer):
    n = z.shape[0]
    return pl.pallas_call(
        _ssd_decode_body,
        grid=(n,),
        in_specs=[_full((n, SSD_INNER)), _full((n, SSD_CH)), _full((n, LANE)),
                  pl.BlockSpec((SSD_CONV - 1, n, SSD_CH), lambda s: (layer, 0, 0)),
                  pl.BlockSpec((1, SSD_INNER, SSD_N), lambda s: (layer * n + s, 0, 0)),
                  _full((SSD_CONV, SSD_CH)), _full((1, SSD_CH)), _full((1, LANE)), _full((1, LANE)),
                  _full((1, SSD_INNER)), _full((1, SSD_INNER))],
        out_specs=[_full((n, SSD_INNER)),
                   pl.BlockSpec((1, SSD_INNER, SSD_N), lambda s: (s, 0, 0)),
                   _full((SSD_CONV - 1, n, SSD_CH))],
        out_shape=[jax.ShapeDtypeStruct((n, SSD_INNER), F32),
                   jax.ShapeDtypeStruct((n, SSD_INNER, SSD_N), F32),
                   jax.ShapeDtypeStruct((SSD_CONV - 1, n, SSD_CH), F32)],
        scratch_shapes=[pltpu.VMEM((SSD_INNER, n), F32), pltpu.VMEM((SSD_INNER, n), F32),
                        pltpu.VMEM((n, 256), F32), pltpu.VMEM((n, 256), F32),
                        pltpu.VMEM((SSD_INNER, n), F32), pltpu.VMEM((n, SSD_INNER), F32)],
        compiler_params=_params(("arbitrary",)),
        name="ssd_decode",
    )(z, xbc, dt, conv_t, h0, w["conv_w"], w["conv_b"], w["dt_bias"], w["a_log"], w["d_skip_e"], w["ssd_norm"])


FF_CHUNK = 1024


def _out_body(x_ref, sb_ref, ssd_ref, mla_ref, sbg_ref, mlag_ref, wo_ref, g1_ref, b1_ref,
              wup_ref, wdn_ref, g2_ref, b2_ref, o_ref):
    sbn = _rms(sb_ref[...], sbg_ref[...]).astype(BF16)
    mlan = _rms(mla_ref[...], mlag_ref[...]).astype(BF16)
    y = (_dot(sbn, wo_ref[0:SB_W, :])
         + _dot(ssd_ref[...].astype(BF16), wo_ref[SB_W:SB_W + SSD_INNER, :])
         + _dot(mlan, wo_ref[SB_W + SSD_INNER:, :]))
    h = _layernorm(ALPHA * x_ref[...] + y, g1_ref[...], b1_ref[...])
    hb = h.astype(BF16)
    f = jnp.zeros(h.shape, F32)
    for j in range(D_FF // FF_CHUNK):
        fs = slice(j * FF_CHUNK, (j + 1) * FF_CHUNK)
        u = jnp.maximum(_dot(hb, wup_ref[:, fs]), 0.0)
        f = f + _dot((u * u).astype(BF16), wdn_ref[fs, :])
    o_ref[...] = _layernorm(ALPHA * h + f, g2_ref[...], b2_ref[...])


def _out_call(x, sb, ssd, mla, w, tm):
    n = x.shape[0]
    row = lambda width: pl.BlockSpec((tm, width), lambda i: (i, 0))
    const = lambda shape: pl.BlockSpec(shape, lambda i: (0,) * len(shape), pipeline_mode=pl.Buffered(1))
    return pl.pallas_call(
        _out_body,
        grid=(n // tm,),
        in_specs=[row(D_MODEL), row(SB_W), row(SSD_INNER), row(MLA_W),
                  const((1, SB_W)), const((1, MLA_W)), const((D_MODEL, D_MODEL)),
                  const((1, D_MODEL)), const((1, D_MODEL)),
                  const((D_MODEL, D_FF)), const((D_FF, D_MODEL)),
                  const((1, D_MODEL)), const((1, D_MODEL))],
        out_specs=row(D_MODEL),
        out_shape=jax.ShapeDtypeStruct((n, D_MODEL), F32),
        compiler_params=_params(("arbitrary",)),
        name="out_ffn",
    )(x, sb, ssd, mla, w["sb_norm"], w["mla_norm"], w["w_out"], w["ln1_g"], w["ln1_b"],
      w["w_up"], w["w_down"], w["ln2_g"], w["ln2_b"])


def _rot_cols(wm):
    half = wm.shape[-1] // 2
    return jnp.concatenate([-wm[..., half:], wm[..., :half]], axis=-1)


def _layer_weights(l, w_in, sb_norm, conv_w, conv_b, dt_bias, a_log, d_skip, ssd_norm, q_norm, w_uq, kv_norm,
                   w_uk, w_uv, mla_norm, w_out, ln1_g, ln1_b, w_up, w_down, ln2_g, ln2_b):
    wi = w_in[l]
    o = [0, 256, 384, 512, 1024, 2048, 2056, 2312, 2568, 2600]
    sq, sk, sv, z, xbc, dtc, cq, ckv, kpe = [wi[:, o[i]:o[i + 1]] for i in range(9)]
    d = wi.shape[0]
    dt_pad = jnp.concatenate([dtc, jnp.zeros((d, LANE - SSD_HEADS), F32)], axis=1)
    kpe_grp = jnp.concatenate([kpe, _rot_cols(kpe), jnp.zeros((d, LANE - 2 * MLA_ROPE), F32)], axis=1)
    w1 = jnp.concatenate([sq, sk, sv, z, xbc, cq, ckv, dt_pad, kpe_grp], axis=1).astype(BF16)

    uq = w_uq[l]
    zq = lambda width: jnp.zeros((MLA_QR, MLA_HEADS, width), F32)
    pe = uq[..., MLA_NOPE:]
    w2a = jnp.concatenate([uq[..., :MLA_NOPE], pe, zq(HEAD_PAD - MLA_NOPE - MLA_ROPE)], axis=-1)
    w2b = jnp.concatenate([zq(MLA_NOPE), _rot_cols(pe), zq(HEAD_PAD - MLA_NOPE - MLA_ROPE)], axis=-1)
    w2 = jnp.concatenate([w2a.reshape(MLA_QR, -1), w2b.reshape(MLA_QR, -1)], axis=1).astype(BF16)

    uk = w_uk[l]
    wk_prompt = jnp.concatenate([uk, jnp.zeros((MLA_KVR, MLA_HEADS, HEAD_PAD - MLA_NOPE), F32)], axis=-1)
    wk_prompt = wk_prompt.reshape(MLA_KVR, MLA_HEADS * HEAD_PAD).astype(BF16)
    ukt = jnp.transpose(uk, (1, 2, 0))
    wk_decode = jnp.concatenate([ukt, jnp.zeros((MLA_HEADS, HEAD_PAD - MLA_NOPE, MLA_KVR), F32)], axis=1)
    wk_decode = wk_decode.reshape(MLA_HEADS * HEAD_PAD, MLA_KVR).astype(BF16)
    wv = w_uv[l].reshape(MLA_KVR, MLA_W).astype(BF16)

    pad_heads = lambda v: jnp.concatenate([v, jnp.zeros((LANE - SSD_HEADS,), F32)])[None, :]
    return {
        "w1": w1, "w2": w2, "wk_prompt": wk_prompt, "wk_decode": wk_decode, "wv": wv, "wv_t": wv.T,
        "q_norm": q_norm[l][None, :], "kv_norm": kv_norm[l][None, :],
        "conv_w": conv_w[l], "conv_b": conv_b[l][None, :],
        "dt_bias": pad_heads(dt_bias[l]), "a_log": pad_heads(a_log[l]),
        "d_skip_e": jnp.repeat(d_skip[l], SSD_P)[None, :], "ssd_norm": ssd_norm[l][None, :],
        "sb_norm": sb_norm[l][None, :], "mla_norm": mla_norm[l][None, :],
        "w_out": w_out[l].astype(BF16), "ln1_g": ln1_g[l][None, :], "ln1_b": ln1_b[l][None, :],
        "w_up": w_up[l].astype(BF16), "w_down": w_down[l].astype(BF16),
        "ln2_g": ln2_g[l][None, :], "ln2_b": ln2_b[l][None, :],
    }


def _rope_table(pos):
    half = MLA_ROPE // 2
    inv = ROPE_THETA ** (-jnp.arange(half, dtype=F32) / half)
    ang = pos.astype(F32)[:, None] * inv[None, :]
    cos, sin = jnp.cos(ang), jnp.sin(ang)
    pad = jnp.zeros((pos.shape[0], LANE - 2 * MLA_ROPE), F32)
    return jnp.concatenate([cos, cos, sin, sin, pad], axis=1)


def _pick(n, candidates):
    for c in candidates:
        if n % c == 0:
            return c
    return n


def kernel(x_prompt, x_sample, cache_sb_k, cache_sb_v, cache_mla_ckv, cache_mla_kpe, state_ssm, state_conv,
           page_table, w_in, sb_norm, conv_w, conv_b, dt_bias, a_log, d_skip, ssd_norm, q_norm, w_uq, kv_norm,
           w_uk, w_uv, mla_norm, w_out, ln1_g, ln1_b, w_up, w_down, ln2_g, ln2_b):
    bp, seq, d = x_prompt.shape
    nd, dec_seq, _ = x_sample.shape
    assert dec_seq == 1 and d == D_MODEL
    depth = w_in.shape[0]
    n_pages = page_table.shape[1]
    past_len = n_pages * PAGE
    n_pool = cache_sb_k.shape[1]

    tab_p = _rope_table(jnp.arange(seq, dtype=jnp.int32))
    tab_s = _rope_table(jnp.full((nd,), past_len, dtype=jnp.int32))
    kc = jnp.transpose(cache_sb_k, (0, 1, 3, 4, 2)).reshape(depth, n_pool, SB_KV * SB_DIM, PAGE)
    vc = jnp.transpose(cache_sb_v, (0, 1, 3, 4, 2)).reshape(depth, n_pool, SB_KV * SB_DIM, PAGE)
    pc = jnp.transpose(cache_mla_kpe, (0, 1, 3, 2))
    conv_all = jnp.transpose(state_conv, (0, 2, 1, 3)).reshape(depth * (SSD_CONV - 1), nd, SSD_CH)
    h0_all = state_ssm.reshape(depth * nd, SSD_INNER, SSD_N)

    tm_p = _pick(seq, (256, 128))
    tq = _pick(seq, (256, 128))
    chunk = _pick(seq, (128,))
    tq_mla = _pick(seq, (2 * tm_p, tm_p))

    hp = x_prompt.reshape(bp * seq, d)
    hs = x_sample.reshape(nd, d)
    new_p, new_s = [], []
    for l in range(depth):
        w = _layer_weights(l, w_in, sb_norm, conv_w, conv_b, dt_bias, a_log, d_skip, ssd_norm, q_norm, w_uq,
                           kv_norm, w_uk, w_uv, mla_norm, w_out, ln1_g, ln1_b, w_up, w_down, ln2_g, ln2_b)
        wp = dict(w, wk=w["wk_prompt"], wv=w["wv_t"])
        (sq, sk, sv, z, xbc, dt, ckv, kpe, kb, vb, qh, kh, vmt) = _proj_call(hp, tab_p, wp, False, tm_p)
        sb_o = _sb_prompt_call(sq, kb, vb, bp, seq, tq)
        mla_o = _mla_prompt_call(qh, kh, vmt, bp, seq, tq_mla, tm_p)
        ssd_o, ssm_new, conv_new = _ssd_prompt_call(z, xbc, dt, w, bp, seq, chunk)
        hp = _out_call(hp, sb_o, ssd_o, mla_o, w, tm_p)
        new_p.append((sk.reshape(bp, seq, SB_KV, SB_DIM), sv.reshape(bp, seq, SB_KV, SB_DIM),
                      ckv.reshape(bp, seq, MLA_KVR), kpe.reshape(bp, seq, MLA_ROPE),
                      ssm_new.reshape(bp, SSD_HEADS, SSD_P, SSD_N), conv_new))
        wd = dict(w, wk=w["wk_decode"])
        (sq, sk, sv, z, xbc, dt, ckv, kpe, qh, qlat) = _proj_call(hs, tab_s, wd, True, nd)
        q4 = sq.reshape(nd, SB_HEADS, SB_DIM)
        zq = jnp.zeros_like(q4)
        grp = (jnp.arange(SB_HEADS) // (SB_HEADS // SB_KV))[None, :, None]
        qbd = jnp.concatenate([jnp.where(grp == 0, q4, zq), jnp.where(grp == 1, q4, zq)], axis=-1)
        qbd = jnp.concatenate([qbd, jnp.zeros_like(qbd)], axis=1)
        sb_o = _sb_decode_call(page_table, qbd, kc, vc, l).reshape(nd, SB_W)
        pad8 = lambda a: jnp.concatenate([a, jnp.zeros_like(a)], axis=1)
        qlat8 = pad8(qlat.reshape(nd, MLA_HEADS, MLA_KVR))
        qpe8 = pad8(qh.reshape(nd, MLA_HEADS, HEAD_PAD)[:, :, MLA_NOPE:MLA_NOPE + MLA_ROPE])
        mla_o = _mla_decode_call(page_table, qlat8, qpe8, ckv[:, None, :], kpe[:, None, :], w["wv"],
                                 cache_mla_ckv, pc, l).reshape(nd, MLA_W)
        ssd_o, ssm_new, conv_new_t = _ssd_decode_call(z, xbc, dt, conv_all, h0_all, w, l)
        hs = _out_call(hs, sb_o, ssd_o, mla_o, w, nd)
        new_s.append((sk.reshape(nd, 1, SB_KV, SB_DIM), sv.reshape(nd, 1, SB_KV, SB_DIM),
                      ckv.reshape(nd, 1, MLA_KVR), kpe.reshape(nd, 1, MLA_ROPE),
                      ssm_new.reshape(nd, SSD_HEADS, SSD_P, SSD_N), jnp.transpose(conv_new_t, (1, 0, 2))))

    stack = lambda states, i: jnp.stack([st[i] for st in states])
    return (hp.reshape(bp, seq, d), hs.reshape(nd, 1, d),
            *[stack(new_p, i) for i in range(6)], *[stack(new_s, i) for i in range(6)])
```

```python
import functools
import math

import jax
import jax.numpy as jnp
from jax import lax
from jax.experimental import pallas as pl
from jax.experimental.pallas import tpu as pltpu

F32 = jnp.float32
BF16 = jnp.bfloat16

D_MODEL = 1024
PAGE = 128
SB_HEADS = 4
SB_KV = 2
SB_DIM = 64
SB_W = SB_HEADS * SB_DIM
SSD_HEADS = 8
SSD_P = 64
SSD_INNER = SSD_HEADS * SSD_P
SSD_GROUPS = 2
SSD_N = 128
SSD_CONV = 4
SSD_CH = SSD_INNER + 2 * SSD_GROUPS * SSD_N
MLA_HEADS = 4
MLA_NOPE = 64
MLA_ROPE = 32
MLA_V = 64
MLA_QR = 256
MLA_KVR = 256
MLA_W = MLA_HEADS * MLA_V
MLA_SCALE = (MLA_NOPE + MLA_ROPE) ** -0.5
LOG2_E = math.log2(math.e)
ROPE_THETA = 10000.0
D_FF = 4 * D_MODEL
EPS = 1e-5
DEPTH = 2
ALPHA = (2 * DEPTH) ** 0.25
LANE = 128
HEAD_PAD = 128

SB_DEAD = -104.0

VMEM_LIMIT = 56 * 1024 * 1024

C_SQ, C_SK, C_SV, C_Z, C_XBC, C_CQ, C_CKV, C_DT, C_KPE, C_END = (
    0, 256, 384, 512, 1024, 2048, 2304, 2560, 2688, 2816)


def _dot(a, b):
    return jnp.dot(a, b, preferred_element_type=F32)


def _dot_nt(a, b):
    return lax.dot_general(a, b, (((1,), (1,)), ((), ())), preferred_element_type=F32)


def _dot_tn(a, b):
    return lax.dot_general(a, b, (((0,), (0,)), ((), ())), preferred_element_type=F32)


def _split_hi_lo(x):
    hi = x.astype(BF16)
    lo = (x - hi.astype(F32)).astype(BF16)
    return hi, lo


def _dot_f32ish(x, w01):
    hi, lo = _split_hi_lo(x)
    return _dot(hi, w01) + _dot(lo, w01)


def _rms(x, g):
    return x * lax.rsqrt(jnp.mean(x * x, axis=-1, keepdims=True) + EPS) * g


def _layernorm(x, g, b):
    mu = jnp.mean(x, axis=-1, keepdims=True)
    xc = x - mu
    var = jnp.mean(xc * xc, axis=-1, keepdims=True)
    return xc * lax.rsqrt(var + EPS) * g + b


def _softplus(x):
    return jnp.maximum(x, 0.0) + jnp.log1p(jnp.exp(-jnp.abs(x)))


def _silu(x):
    return x / (1.0 + jnp.exp(-x))


def _full(shape):
    nd = len(shape)
    return pl.BlockSpec(shape, lambda *_: (0,) * nd)


def _params(sem):
    return pltpu.CompilerParams(dimension_semantics=sem, vmem_limit_bytes=VMEM_LIMIT)


def _proj_body(decode, x_ref, tab_ref, w1_ref, w2_ref, wk_ref, wv_ref, qg_ref, kvg_ref, *outs):
    if decode:
        (sq_ref, sk_ref, sv_ref, z_ref, xbc_ref, dt_ref, ckv_ref, kpe_ref, qh_ref, qlat_ref) = outs
    else:
        (sq_ref, sk_ref, sv_ref, z_ref, xbc_ref, dt_ref, ckv_ref, kpe_ref,
         kb_ref, vb_ref, qh_ref, kh_ref, vm_ref) = outs
    x = x_ref[...].astype(BF16)

    def col(lo, hi):
        return _dot(x, w1_ref[:, lo:hi])

    sq_ref[...] = col(C_SQ, C_SK)
    sk = col(C_SK, C_SV)
    sv = col(C_SV, C_Z)
    sk_ref[...] = sk
    sv_ref[...] = sv
    z_ref[...] = col(C_Z, C_XBC)
    xbc_ref[...] = col(C_XBC, C_CQ)
    dt_ref[...] = col(C_DT, C_KPE)

    tab = tab_ref[...]
    lane = lax.broadcasted_iota(jnp.int32, tab.shape, 1)
    rope_q = jnp.logical_and(lane >= MLA_NOPE, lane < MLA_NOPE + MLA_ROPE)
    cosq = jnp.where(lane < MLA_NOPE, 1.0, jnp.where(rope_q, pltpu.roll(tab, 64, 1), 0.0))
    sinq = jnp.where(rope_q, pltpu.roll(tab, 32, 1), 0.0)
    cosk = jnp.where(lane < MLA_ROPE, tab, 0.0)
    sink = jnp.where(lane < MLA_ROPE, pltpu.roll(tab, 96, 1), 0.0)

    cqn = _rms(col(C_CQ, C_CKV), qg_ref[...]).astype(BF16)
    qa = _dot(cqn, w2_ref[:, 0:512])
    qb = _dot(cqn, w2_ref[:, 512:1024])
    q_scale = MLA_SCALE if decode else MLA_SCALE * LOG2_E
    qh = []
    for h in range(MLA_HEADS):
        sl = slice(h * HEAD_PAD, (h + 1) * HEAD_PAD)
        qh.append((qa[:, sl] * cosq + qb[:, sl] * sinq) * q_scale)

    ckvn = _rms(col(C_CKV, C_DT), kvg_ref[...])
    ckv_ref[...] = ckvn
    kg = col(C_KPE, C_END)
    kpe = kg * cosk + pltpu.roll(kg, 96, 1) * sink
    kpe_ref[...] = kpe[:, 0:MLA_ROPE]

    if decode:
        for h in range(MLA_HEADS):
            sl = slice(h * HEAD_PAD, (h + 1) * HEAD_PAD)
            qh_ref[:, sl] = qh[h]
            qlat_ref[:, h * MLA_KVR:(h + 1) * MLA_KVR] = _dot(qh[h].astype(BF16), wk_ref[sl, :])
    else:
        kb_ref[...] = sk.astype(BF16)
        vb_ref[...] = sv.astype(BF16)
        ckvb = ckvn.astype(BF16)
        kpe_at64 = pltpu.roll(kpe, 64, 1)
        for h in range(MLA_HEADS):
            sl = slice(h * HEAD_PAD, (h + 1) * HEAD_PAD)
            qh_ref[:, sl] = qh[h].astype(BF16)
            kh_ref[:, sl] = (_dot(ckvb, wk_ref[:, sl]) + kpe_at64).astype(BF16)
        vm_ref[0] = _dot_nt(wv_ref[...], ckvb).astype(BF16)


def _proj_call(x, tab, w, decode, tm):
    n = x.shape[0]
    nt = tab.shape[0] // tm
    grid = (n // tm,)
    row = lambda width: pl.BlockSpec((tm, width), lambda i: (i, 0))
    in_specs = [
        row(D_MODEL),
        pl.BlockSpec((tm, LANE), lambda i: (i % nt, 0)),
        _full(w["w1"].shape), _full(w["w2"].shape), _full(w["wk"].shape), _full(w["wv"].shape),
        _full((1, MLA_QR)), _full((1, MLA_KVR)),
    ]
    f = lambda width, dt=F32: jax.ShapeDtypeStruct((n, width), dt)
    out_shape = [f(256), f(128), f(128), f(512), f(1024), f(128), f(256), f(MLA_ROPE)]
    out_specs = [row(256), row(128), row(128), row(512), row(1024), row(128), row(256), row(MLA_ROPE)]
    if decode:
        out_shape += [f(512), f(MLA_HEADS * MLA_KVR)]
        out_specs += [row(512), row(MLA_HEADS * MLA_KVR)]
    else:
        out_shape += [f(128, BF16), f(128, BF16), f(512, BF16), f(512, BF16),
                      jax.ShapeDtypeStruct((n // tm, MLA_W, tm), BF16)]
        out_specs += [row(128), row(128), row(512), row(512),
                      pl.BlockSpec((1, MLA_W, tm), lambda i: (i, 0, 0))]
    return pl.pallas_call(
        functools.partial(_proj_body, decode),
        grid=grid, in_specs=in_specs, out_specs=out_specs, out_shape=out_shape,
        compiler_params=_params(("arbitrary",)),
        name="proj_decode" if decode else "proj_prompt",
    )(x, tab, w["w1"], w["w2"], w["wk"], w["wv"], w["q_norm"], w["kv_norm"])


def _suffix_matrix(tk):
    j = lax.broadcasted_iota(jnp.int32, (tk, tk), 0)
    s = lax.broadcasted_iota(jnp.int32, (tk, tk), 1)
    return jnp.where(j > s, 1.0, 0.0).astype(BF16)


def _sb_block(qall, kblk, vblk, umat, c, mask, keys_on_lanes=False):
    z = (_dot(qall, kblk) if keys_on_lanes else _dot_nt(qall, kblk)) * (SB_DIM ** -0.5)
    lse = jnp.log1p(jnp.exp(-jnp.abs(z)))
    log_beta = jnp.minimum(z, 0.0) - lse
    log_keep = log_beta - z
    if mask is not None:
        log_keep = jnp.where(mask, log_keep, 0.0)
    hi, lo = _split_hi_lo(log_keep)
    after = _dot(hi, umat) + _dot(lo, umat)
    w = jnp.exp(log_beta + after + c)
    if mask is not None:
        w = jnp.where(mask, w, 0.0)
    wb = w.astype(BF16)
    pv = _dot_nt(wb, vblk) if keys_on_lanes else _dot(wb, vblk)
    return pv, c + jnp.sum(log_keep, axis=1, keepdims=True)


def _sb_prompt_body(q_ref, k_ref, v_ref, o_ref, qall_ref, acc_ref, c_ref, *, tq):
    i = pl.program_id(1)
    tk = tq
    lane = lax.broadcasted_iota(jnp.int32, (tq, LANE), 1)
    lo_half = lane < SB_DIM
    s0 = q_ref[:, 0:128]
    s1 = q_ref[:, 128:256]
    qall_ref[0 * tq:1 * tq, :] = jnp.where(lo_half, s0, 0.0).astype(BF16)
    qall_ref[1 * tq:2 * tq, :] = jnp.where(lo_half, pltpu.roll(s0, 64, 1), 0.0).astype(BF16)
    qall_ref[2 * tq:3 * tq, :] = jnp.where(lo_half, 0.0, pltpu.roll(s1, 64, 1)).astype(BF16)
    qall_ref[3 * tq:4 * tq, :] = jnp.where(lo_half, 0.0, s1).astype(BF16)
    umat = _suffix_matrix(tk)

    row = lax.rem(lax.broadcasted_iota(jnp.int32, (SB_HEADS * tq, tk), 0), tq)
    colm = lax.broadcasted_iota(jnp.int32, (SB_HEADS * tq, tk), 1)
    mask = colm < row

    def run(kb, c, m):
        start = pl.multiple_of(kb * tk, tk)
        return _sb_block(qall_ref[...], k_ref[pl.ds(start, tk), :], v_ref[pl.ds(start, tk), :], umat, c, m)

    pv, c = run(i, jnp.zeros((SB_HEADS * tq, 1), F32), mask)
    acc_ref[...] = pv
    c_ref[...] = c

    def cond(st):
        kb, cmax = st
        return jnp.logical_and(kb >= 0, cmax > SB_DEAD)

    def body(st):
        kb, _ = st
        pv, c = run(kb, c_ref[...], None)
        acc_ref[...] += pv
        c_ref[...] = c
        return kb - 1, jnp.max(c)

    lax.while_loop(cond, body, (i - 1, jnp.max(c)))

    o0 = acc_ref[0 * tq:1 * tq, :]
    o1 = acc_ref[1 * tq:2 * tq, :]
    o2 = acc_ref[2 * tq:3 * tq, :]
    o3 = acc_ref[3 * tq:4 * tq, :]
    o_ref[:, 0:128] = jnp.where(lo_half, o0, pltpu.roll(o1, 64, 1))
    o_ref[:, 128:256] = jnp.where(lo_half, pltpu.roll(o2, 64, 1), o3)


def _sb_prompt_call(sq, kb, vb, bsz, seq, tq):
    nq = seq // tq
    return pl.pallas_call(
        functools.partial(_sb_prompt_body, tq=tq),
        grid=(bsz, nq),
        in_specs=[
            pl.BlockSpec((tq, SB_W), lambda b, i: (b * nq + i, 0)),
            pl.BlockSpec((seq, 128), lambda b, i: (b, 0)),
            pl.BlockSpec((seq, 128), lambda b, i: (b, 0)),
        ],
        out_specs=pl.BlockSpec((tq, SB_W), lambda b, i: (b * nq + i, 0)),
        out_shape=jax.ShapeDtypeStruct((bsz * seq, SB_W), F32),
        scratch_shapes=[
            pltpu.VMEM((SB_HEADS * tq, LANE), BF16),
            pltpu.VMEM((SB_HEADS * tq, LANE), F32),
            pltpu.VMEM((SB_HEADS * tq, 1), F32),
        ],
        compiler_params=_params(("arbitrary", "arbitrary")),
        name="sb_prompt",
    )(sq, kb, vb)


SB_DEC_PAGES = 2


def _sb_decode_body(pt_ref, q_ref, kc_ref, vc_ref, o_ref, kbuf, vbuf, sem, *, layer, n_pages):
    s = pl.program_id(0)
    g_pages = SB_DEC_PAGES
    ng = n_pages // g_pages
    tk = g_pages * PAGE

    def copies(seq, g, slot):
        out = []
        for j in range(g_pages):
            page = pt_ref[seq, g * g_pages + j]
            dst = pl.ds(j * PAGE, PAGE)
            out.append(pltpu.make_async_copy(kc_ref.at[layer, page], kbuf.at[slot, :, dst], sem.at[0, slot]))
            out.append(pltpu.make_async_copy(vc_ref.at[layer, page], vbuf.at[slot, :, dst], sem.at[1, slot]))
        return out

    def start(seq, g, slot):
        for cp in copies(seq, g, slot):
            cp.start()

    def wait(seq, g, slot):
        for cp in copies(seq, g, slot):
            cp.wait()

    newest_slot = lambda seq: 2 + lax.rem(seq, 2)
    slot_of = lambda g: lax.rem(ng - 2 - g, 2)
    umat = _suffix_matrix(tk)
    qall = q_ref[0].astype(BF16)

    def step(slot, c, acc):
        pv, c = _sb_block(qall, kbuf[slot].astype(BF16), vbuf[slot].astype(BF16), umat, c, None,
                          keys_on_lanes=True)
        return jnp.max(c[0:SB_HEADS]), c, acc + pv

    @pl.when(s == 0)
    def _():
        start(s, ng - 1, newest_slot(s))

    @pl.when(s + 1 < pl.num_programs(0))
    def _():
        start(s + 1, ng - 1, newest_slot(s + 1))

    if ng >= 2:
        start(s, ng - 2, 0)
    wait(s, ng - 1, newest_slot(s))
    first = step(newest_slot(s), jnp.zeros((8, 1), F32), jnp.zeros((8, LANE), F32))

    def cond(st):
        g, cmax, _, _ = st
        return jnp.logical_and(g >= 0, cmax > SB_DEAD)

    def body(st):
        g, _, c, acc = st
        slot = slot_of(g)

        @pl.when(g >= 1)
        def _():
            start(s, g - 1, 1 - slot)

        wait(s, g, slot)
        return (g - 1, *step(slot, c, acc))

    g_end, _, _, acc = lax.while_loop(cond, body, (jnp.int32(ng - 2), *first))

    @pl.when(g_end >= 0)
    def _():
        wait(s, g_end, slot_of(g_end))

    lane = lax.broadcasted_iota(jnp.int32, (1, LANE), 1)
    lo_half = lane < SB_DIM
    o_ref[0, :, 0:128] = jnp.where(lo_half, acc[0:1], pltpu.roll(acc[1:2], 64, 1))
    o_ref[0, :, 128:256] = jnp.where(lo_half, pltpu.roll(acc[2:3], 64, 1), acc[3:4])


def _sb_decode_call(page_table, qbd, kc, vc, layer):
    n, n_pages = page_table.shape
    grid_spec = pltpu.PrefetchScalarGridSpec(
        num_scalar_prefetch=1,
        grid=(n,),
        in_specs=[
            pl.BlockSpec((1, 8, LANE), lambda s, pt: (s, 0, 0)),
            pl.BlockSpec(memory_space=pl.ANY),
            pl.BlockSpec(memory_space=pl.ANY),
        ],
        out_specs=pl.BlockSpec((1, 1, SB_W), lambda s, pt: (s, 0, 0)),
        scratch_shapes=[
            pltpu.VMEM((4, LANE, SB_DEC_PAGES * PAGE), F32),
            pltpu.VMEM((4, LANE, SB_DEC_PAGES * PAGE), F32),
            pltpu.SemaphoreType.DMA((2, 4)),
        ],
    )
    return pl.pallas_call(
        functools.partial(_sb_decode_body, layer=layer, n_pages=n_pages),
        grid_spec=grid_spec,
        out_shape=jax.ShapeDtypeStruct((n, 1, SB_W), F32),
        compiler_params=_params(("arbitrary",)),
        name="sb_decode",
    )(page_table, qbd, kc, vc)


def _mla_prompt_body(q_ref, k_ref, vt_ref, o_ref, *, tq, tk):
    i = pl.program_id(1)
    ratio = tq // tk
    key = lax.broadcasted_iota(jnp.int32, (tk, tq), 0)
    qry = lax.broadcasted_iota(jnp.int32, (tk, tq), 1)

    heads = range(MLA_HEADS)

    def scores(kb):
        start = pl.multiple_of(kb * tk, tk)
        return tuple(_dot_nt(k_ref[pl.ds(start, tk), h * HEAD_PAD:(h + 1) * HEAD_PAD],
                             q_ref[:, h * HEAD_PAD:(h + 1) * HEAD_PAD]) for h in heads)

    def absorb(kb, sts, state, mask):
        ps, ms, ls, alphas = [], [], [], []
        for h in heads:
            m_old, l_old, _ = state[h]
            st = sts[h] if mask is None else jnp.where(mask, sts[h], -jnp.inf)
            m_new = jnp.maximum(m_old, jnp.max(st, axis=0, keepdims=True))
            p = jnp.exp2(st - m_new)
            alpha = jnp.exp2(m_old - m_new)
            ls.append(alpha * l_old + jnp.sum(p, axis=0, keepdims=True))
            ms.append(m_new)
            alphas.append(alpha)
            ps.append(p.astype(BF16))
        pvs = [_dot(vt_ref[kb, h * MLA_V:(h + 1) * MLA_V, :], ps[h]) for h in heads]
        return tuple((ms[h], ls[h], alphas[h] * state[h][2] + pvs[h]) for h in heads)

    init = tuple((jnp.full((1, tq), -jnp.inf, F32), jnp.zeros((1, tq), F32), jnp.zeros((MLA_V, tq), F32))
                 for _ in range(MLA_HEADS))
    n_full = i * ratio

    def trip(kb0, state, masks):
        sts = [scores(kb0 + j) for j in range(ratio)]
        for j in range(ratio):
            state = absorb(kb0 + j, sts[j], state, masks[j])
        return state

    state = lax.fori_loop(0, i, lambda t, st: trip(t * ratio, st, [None] * ratio), init)
    state = trip(n_full, state, [key + j * tk <= qry for j in range(ratio)])
    o_t = jnp.concatenate([acc / l for (_, l, acc) in state], axis=0)
    o_ref[...] = o_t.T


def _mla_prompt_call(qh, kh, vmt, bsz, seq, tq, tk):
    nq = seq // tq
    nk = seq // tk
    return pl.pallas_call(
        functools.partial(_mla_prompt_body, tq=tq, tk=tk),
        grid=(bsz, nq),
        in_specs=[
            pl.BlockSpec((tq, 512), lambda b, i: (b * nq + i, 0)),
            pl.BlockSpec((seq, 512), lambda b, i: (b, 0)),
            pl.BlockSpec((nk, MLA_W, tk), lambda b, i: (b, 0, 0)),
        ],
        out_specs=pl.BlockSpec((tq, MLA_W), lambda b, i: (b * nq + i, 0)),
        out_shape=jax.ShapeDtypeStruct((bsz * seq, MLA_W), F32),
        compiler_params=_params(("arbitrary", "arbitrary")),
        name="mla_prompt",
    )(qh, kh, vmt)


MLA_DEC_PAGES = 32
MLA_DEC_BUFS = 4


def _mla_decode_body(pt_ref, ql_ref, qp_ref, cn_ref, pn_ref, wuv_ref, cc_ref, pc_ref, o_ref,
                     cbuf, pbuf, cb16, sc_keep, sem, *, layer, n_pages):
    s = pl.program_id(0)
    n = pl.num_programs(0)
    g_pages = MLA_DEC_PAGES
    ng = n_pages // g_pages
    nbuf = MLA_DEC_BUFS
    total = n * ng

    def copies(t):
        seq_raw = lax.div(t, jnp.int32(ng))
        g = t - seq_raw * ng
        seq = lax.rem(seq_raw, n)
        slot = lax.rem(t, jnp.int32(nbuf))
        out = []
        for j in range(g_pages):
            page = pt_ref[seq, g * g_pages + j]
            dst = pl.ds(j * PAGE, PAGE)
            out.append(pltpu.make_async_copy(cc_ref.at[layer, page], cbuf.at[slot, dst], sem.at[0, slot]))
            out.append(pltpu.make_async_copy(pc_ref.at[layer, page], pbuf.at[slot, :, dst], sem.at[1, slot]))
        return out

    def start(t):
        for cp in copies(t):
            cp.start()

    def wait(t):
        for cp in copies(t):
            cp.wait()

    def scores(t, seq):
        wait(t)
        slot = lax.rem(t, jnp.int32(nbuf))
        cb = cbuf[slot].astype(BF16)
        cb16[lax.rem(t, jnp.int32(2))] = cb
        return (_dot_nt(ql_ref[seq].astype(BF16), cb)
                + _dot(qp_ref[seq].astype(BF16), pbuf[slot].astype(BF16)))

    def refill(t):
        @pl.when(t + nbuf <= total)
        def _():
            start(t + nbuf)

    def absorb(t, sc, st):
        m_old, l_old, acc = st
        m_new = jnp.maximum(m_old, jnp.max(sc, axis=1, keepdims=True))
        p = jnp.exp(sc - m_new)
        alpha = jnp.exp(m_old - m_new)
        pv = _dot(p.astype(BF16), cb16[lax.rem(t, jnp.int32(2))])
        return m_new, alpha * l_old + jnp.sum(p, axis=1, keepdims=True), alpha * acc + pv

    @pl.when(s == 0)
    def _():
        for t0 in range(nbuf):
            start(t0)
        sc_keep[...] = scores(0, 0)

    t_first = s * ng

    def body(g, carry):
        sc, st = carry
        t = t_first + g
        sc_next = scores(t + 1, s)
        st = absorb(t, sc, st)
        refill(t)
        return sc_next, st

    init = (jnp.full((8, 1), -jnp.inf, F32), jnp.zeros((8, 1), F32), jnp.zeros((8, MLA_KVR), F32))
    sc, st = lax.fori_loop(0, ng - 1, body, (sc_keep[...], init))
    t_last = t_first + ng - 1
    sc_keep[...] = scores(t_last + 1, lax.rem(s + 1, n))
    m_old, l_old, acc = absorb(t_last, sc, st)
    refill(t_last)
    ql = ql_ref[s]
    qp = qp_ref[s]

    cn = cn_ref[0]
    pn = pn_ref[0]
    sc = jnp.sum(ql * cn, axis=1, keepdims=True) + jnp.sum(qp * pn, axis=1, keepdims=True)
    m_new = jnp.maximum(m_old, sc)
    p = jnp.exp(sc - m_new)
    alpha = jnp.exp(m_old - m_new)
    l_new = alpha * l_old + p
    o_lat = (alpha * acc + p * cn) / l_new
    o_all = _dot(o_lat.astype(BF16), wuv_ref[...])
    sub = lax.broadcasted_iota(jnp.int32, (8, MLA_W), 0)
    lane = lax.broadcasted_iota(jnp.int32, (8, MLA_W), 1)
    pick = (lane // MLA_V) == sub
    o_ref[0] = jnp.sum(jnp.where(pick, o_all, 0.0), axis=0, keepdims=True)


def _mla_decode_call(page_table, qlat, qpe, ckv_new, kpe_new, wuv, cc, pc, layer):
    n, n_pages = page_table.shape
    tk = MLA_DEC_PAGES * PAGE
    grid_spec = pltpu.PrefetchScalarGridSpec(
        num_scalar_prefetch=1,
        grid=(n,),
        in_specs=[
            pl.BlockSpec((n, 8, MLA_KVR), lambda s, pt: (0, 0, 0)),
            pl.BlockSpec((n, 8, MLA_ROPE), lambda s, pt: (0, 0, 0)),
            pl.BlockSpec((1, 1, MLA_KVR), lambda s, pt: (s, 0, 0)),
            pl.BlockSpec((1, 1, MLA_ROPE), lambda s, pt: (s, 0, 0)),
            pl.BlockSpec((MLA_KVR, MLA_W), lambda s, pt: (0, 0)),
            pl.BlockSpec(memory_space=pl.ANY),
            pl.BlockSpec(memory_space=pl.ANY),
        ],
        out_specs=pl.BlockSpec((1, 1, MLA_W), lambda s, pt: (s, 0, 0)),
        scratch_shapes=[
            pltpu.VMEM((MLA_DEC_BUFS, tk, MLA_KVR), F32),
            pltpu.VMEM((MLA_DEC_BUFS, MLA_ROPE, tk), F32),
            pltpu.VMEM((2, tk, MLA_KVR), BF16),
            pltpu.VMEM((8, tk), F32),
            pltpu.SemaphoreType.DMA((2, MLA_DEC_BUFS)),
        ],
    )
    assert n_pages % MLA_DEC_PAGES == 0 and n * (n_pages // MLA_DEC_PAGES) >= MLA_DEC_BUFS
    return pl.pallas_call(
        functools.partial(_mla_decode_body, layer=layer, n_pages=n_pages),
        grid_spec=grid_spec,
        out_shape=jax.ShapeDtypeStruct((n, 1, MLA_W), F32),
        compiler_params=_params(("arbitrary",)),
        name="mla_decode",
    )(page_table, qlat, qpe, ckv_new, kpe_new, wuv, cc, pc)


def _expand_matrix():
    r = lax.broadcasted_iota(jnp.int32, (LANE, SSD_INNER), 0)
    c = lax.broadcasted_iota(jnp.int32, (LANE, SSD_INNER), 1)
    return jnp.where(c // SSD_P == r, 1.0, 0.0).astype(BF16)


def _gated_norm(y, z, norm_w):
    g = y * _silu(z)
    half = SSD_INNER // SSD_GROUPS
    parts = []
    for k in range(SSD_GROUPS):
        gk = g[:, k * half:(k + 1) * half]
        parts.append(gk * lax.rsqrt(jnp.mean(gk * gk, axis=-1, keepdims=True) + EPS))
    return jnp.concatenate(parts, axis=1) * norm_w


def _ssd_prompt_body(z_ref, xbc_ref, dt_ref, cw_ref, cb_ref, dtb_ref, alog_ref, dskip_ref, nw_ref,
                     y_ref, hout_ref, cout_ref, xwin, ht, *, chunk):
    c = pl.program_id(1)
    nc = pl.num_programs(1)
    L = chunk

    @pl.when(c == 0)
    def _():
        xwin[0:8, :] = jnp.zeros((8, SSD_CH), F32)
        ht[...] = jnp.zeros(ht.shape, F32)

    xwin[8:8 + L, :] = xbc_ref[...]
    conv = cb_ref[...] + xwin[5:5 + L, :] * cw_ref[0:1, :]
    for i in range(1, SSD_CONV):
        conv = conv + xwin[5 + i:5 + i + L, :] * cw_ref[i:i + 1, :]
    xwin[0:8, :] = xwin[L:L + 8, :]
    cv = _silu(conv)
    xs = cv[:, 0:SSD_INNER]
    bm = cv[:, SSD_INNER:SSD_INNER + 256]
    cm = cv[:, SSD_INNER + 256:SSD_INNER + 512]

    dt = _softplus(dt_ref[...] + dtb_ref[...])
    da = dt * (-jnp.exp(alog_ref[...]))
    r = lax.broadcasted_iota(jnp.int32, (L, L), 0)
    s = lax.broadcasted_iota(jnp.int32, (L, L), 1)
    causal = s <= r
    tri = jnp.where(causal, 1.0, 0.0).astype(BF16)
    cs = _dot_f32ish_left(tri, da)
    cs_t = cs.T
    cs_last = cs[L - 1:L, :]
    ecs = jnp.exp(cs)
    emat = _expand_matrix()
    dt_e = _dot_f32ish(dt, emat)
    ecs_e = _dot_f32ish(ecs, emat)
    te_e = _dot_f32ish(dt * jnp.exp(cs_last - cs), emat)
    last_e = _dot_f32ish(jnp.exp(cs_last), emat)

    x_dt = (xs * dt_e).astype(BF16)
    x_end = (xs * te_e).astype(BF16)
    lane = lax.broadcasted_iota(jnp.int32, (L, LANE), 1)
    lo_half = lane < SSD_P
    y_parts = []
    for g in range(SSD_GROUPS):
        gs = slice(g * SSD_N, (g + 1) * SSD_N)
        cg = cm[:, gs].astype(BF16)
        bg = bm[:, gs].astype(BF16)
        cb = _dot_nt(cg, bg)
        hs = slice(g * 256, (g + 1) * 256)
        h_in = ht[g]
        y_off = _dot(cg, h_in.astype(BF16)) * ecs_e[:, hs]
        yd = []
        for hh in range(4):
            h = g * 4 + hh
            seg = cs[:, h:h + 1] - cs_t[h:h + 1, :]
            decay = jnp.exp(jnp.where(causal, seg, -jnp.inf))
            sc = (cb * decay).astype(BF16)
            ps = slice((h // 2) * LANE, (h // 2 + 1) * LANE)
            yd.append(_dot(sc, x_dt[:, ps]))
        y_diag = jnp.concatenate([jnp.where(lo_half, yd[0], yd[1]), jnp.where(lo_half, yd[2], yd[3])], axis=1)
        y_parts.append(y_diag + y_off)
        ht[g] = h_in * last_e[:, hs] + _dot_tn(bg, x_end[:, hs])
    y = jnp.concatenate(y_parts, axis=1) + dskip_ref[...] * xs
    y_ref[...] = _gated_norm(y, z_ref[...], nw_ref[...])

    @pl.when(c == nc - 1)
    def _():
        for g in range(SSD_GROUPS):
            hout_ref[0, g * 256:(g + 1) * 256, :] = ht[g].T
        cout_ref[0] = xwin[0:8, :][5:8, :]


def _dot_f32ish_left(w01, x):
    hi, lo = _split_hi_lo(x)
    return _dot(w01, hi) + _dot(w01, lo)


def _ssd_prompt_call(z, xbc, dt, w, bsz, seq, chunk):
    nc = seq // chunk
    row = lambda width: pl.BlockSpec((chunk, width), lambda b, c: (b * nc + c, 0))
    return pl.pallas_call(
        functools.partial(_ssd_prompt_body, chunk=chunk),
        grid=(bsz, nc),
        in_specs=[row(SSD_INNER), row(SSD_CH), row(LANE),
                  _full((SSD_CONV, SSD_CH)), _full((1, SSD_CH)), _full((1, LANE)), _full((1, LANE)),
                  _full((1, SSD_INNER)), _full((1, SSD_INNER))],
        out_specs=[row(SSD_INNER),
                   pl.BlockSpec((1, SSD_INNER, SSD_N), lambda b, c: (b, 0, 0)),
                   pl.BlockSpec((1, SSD_CONV - 1, SSD_CH), lambda b, c: (b, 0, 0))],
        out_shape=[jax.ShapeDtypeStruct((bsz * seq, SSD_INNER), F32),
                   jax.ShapeDtypeStruct((bsz, SSD_INNER, SSD_N), F32),
                   jax.ShapeDtypeStruct((bsz, SSD_CONV - 1, SSD_CH), F32)],
        scratch_shapes=[pltpu.VMEM((chunk + 8, SSD_CH), F32), pltpu.VMEM((SSD_GROUPS, SSD_N, 256), F32)],
        compiler_params=_params(("arbitrary", "arbitrary")),
        name="ssd_prompt",
    )(z, xbc, dt, w["conv_w"], w["conv_b"], w["dt_bias"], w["a_log"], w["d_skip_e"], w["ssd_norm"])


def _ssd_decode_body(z_ref, xbc_ref, dt_ref, cbuf_ref, h0_ref, cw_ref, cb_ref, dtb_ref, alog_ref, dskip_ref,
                     nw_ref, y_ref, hout_ref, cout_ref, xt, dect, bsel, csel, yt, xs_keep):
    s = pl.program_id(0)
    n = pl.num_programs(0)

    @pl.when(s == 0)
    def _():
        xnew = xbc_ref[...]
        conv = cb_ref[...] + xnew * cw_ref[3:4, :]
        for i in range(SSD_CONV - 1):
            conv = conv + cbuf_ref[i] * cw_ref[i:i + 1, :]
        cout_ref[0] = cbuf_ref[1]
        cout_ref[1] = cbuf_ref[2]
        cout_ref[2] = xnew
        cv = _silu(conv)
        xs = cv[:, 0:SSD_INNER]
        xs_keep[...] = xs
        bsel[...] = cv[:, SSD_INNER:SSD_INNER + 256]
        csel[...] = cv[:, SSD_INNER + 256:SSD_INNER + 512]
        dt = _softplus(dt_ref[...] + dtb_ref[...])
        da = dt * (-jnp.exp(alog_ref[...]))
        emat = _expand_matrix()
        xt[...] = (xs * _dot_f32ish(dt, emat)).T
        dect[...] = _dot_f32ish(jnp.exp(da), emat).T
        yt[...] = jnp.zeros(yt.shape, F32)

    lane = lax.broadcasted_iota(jnp.int32, xt.shape, 1)
    here = lane == s
    x_col = jnp.sum(jnp.where(here, xt[...], 0.0), axis=1, keepdims=True)
    d_col = jnp.sum(jnp.where(here, dect[...], 0.0), axis=1, keepdims=True)
    b_row = bsel[pl.ds(s, 1), :]
    c_row = csel[pl.ds(s, 1), :]
    ys = []
    for g in range(SSD_GROUPS):
        hs = slice(g * 256, (g + 1) * 256)
        gs = slice(g * SSD_N, (g + 1) * SSD_N)
        h_new = h0_ref[0, hs, :] * d_col[hs] + x_col[hs] * b_row[:, gs]
        hout_ref[0, hs, :] = h_new
        ys.append(jnp.sum(h_new * c_row[:, gs], axis=1, keepdims=True))
    y_col = jnp.concatenate(ys, axis=0)
    yt[...] = jnp.where(here, y_col, yt[...])

    @pl.when(s == n - 1)
    def _():
        xs = xs_keep[...]
        y = yt[...].T + dskip_ref[...] * xs
        y_ref[...] = _gated_norm(y, z_ref[...], nw_ref[...])


def _ssd_decode_call(z, xbc, dt, conv_t, h0, w, layer):
    n = z.shape[0]
    return pl.pallas_call(
        _ssd_decode_body,
        grid=(n,),
        in_specs=[_full((n, SSD_INNER)), _full((n, SSD_CH)), _full((n, LANE)),
                  pl.BlockSpec((SSD_CONV - 1, n, SSD_CH), lambda s: (layer, 0, 0)),
                  pl.BlockSpec((1, SSD_INNER, SSD_N), lambda s: (layer * n + s, 0, 0)),
                  _full((SSD_CONV, SSD_CH)), _full((1, SSD_CH)), _full((1, LANE)), _full((1, LANE)),
                  _full((1, SSD_INNER)), _full((1, SSD_INNER))],
        out_specs=[_full((n, SSD_INNER)),
                   pl.BlockSpec((1, SSD_INNER, SSD_N), lambda s: (s, 0, 0)),
                   _full((SSD_CONV - 1, n, SSD_CH))],
        out_shape=[jax.ShapeDtypeStruct((n, SSD_INNER), F32),
                   jax.ShapeDtypeStruct((n, SSD_INNER, SSD_N), F32),
                   jax.ShapeDtypeStruct((SSD_CONV - 1, n, SSD_CH), F32)],
        scratch_shapes=[pltpu.VMEM((SSD_INNER, n), F32), pltpu.VMEM((SSD_INNER, n), F32),
                        pltpu.VMEM((n, 256), F32), pltpu.VMEM((n, 256), F32),
                        pltpu.VMEM((SSD_INNER, n), F32), pltpu.VMEM((n, SSD_INNER), F32)],
        compiler_params=_params(("arbitrary",)),
        name="ssd_decode",
    )(z, xbc, dt, conv_t, h0, w["conv_w"], w["conv_b"], w["dt_bias"], w["a_log"], w["d_skip_e"], w["ssd_norm"])


FF_CHUNK = 1024


def _out_body(x_ref, sb_ref, ssd_ref, mla_ref, sbg_ref, mlag_ref, wo_ref, g1_ref, b1_ref,
              wup_ref, wdn_ref, g2_ref, b2_ref, o_ref):
    sbn = _rms(sb_ref[...], sbg_ref[...]).astype(BF16)
    mlan = _rms(mla_ref[...], mlag_ref[...]).astype(BF16)
    y = (_dot(sbn, wo_ref[0:SB_W, :])
         + _dot(ssd_ref[...].astype(BF16), wo_ref[SB_W:SB_W + SSD_INNER, :])
         + _dot(mlan, wo_ref[SB_W + SSD_INNER:, :]))
    h = _layernorm(ALPHA * x_ref[...] + y, g1_ref[...], b1_ref[...])
    hb = h.astype(BF16)
    f = jnp.zeros(h.shape, F32)
    for j in range(D_FF // FF_CHUNK):
        fs = slice(j * FF_CHUNK, (j + 1) * FF_CHUNK)
        u = jnp.maximum(_dot(hb, wup_ref[:, fs]), 0.0)
        f = f + _dot((u * u).astype(BF16), wdn_ref[fs, :])
    o_ref[...] = _layernorm(ALPHA * h + f, g2_ref[...], b2_ref[...])


def _out_call(x, sb, ssd, mla, w, tm):
    n = x.shape[0]
    row = lambda width: pl.BlockSpec((tm, width), lambda i: (i, 0))
    const = lambda shape: pl.BlockSpec(shape, lambda i: (0,) * len(shape), pipeline_mode=pl.Buffered(1))
    return pl.pallas_call(
        _out_body,
        grid=(n // tm,),
        in_specs=[row(D_MODEL), row(SB_W), row(SSD_INNER), row(MLA_W),
                  const((1, SB_W)), const((1, MLA_W)), const((D_MODEL, D_MODEL)),
                  const((1, D_MODEL)), const((1, D_MODEL)),
                  const((D_MODEL, D_FF)), const((D_FF, D_MODEL)),
                  const((1, D_MODEL)), const((1, D_MODEL))],
        out_specs=row(D_MODEL),
        out_shape=jax.ShapeDtypeStruct((n, D_MODEL), F32),
        compiler_params=_params(("arbitrary",)),
        name="out_ffn",
    )(x, sb, ssd, mla, w["sb_norm"], w["mla_norm"], w["w_out"], w["ln1_g"], w["ln1_b"],
      w["w_up"], w["w_down"], w["ln2_g"], w["ln2_b"])


def _rot_cols(wm):
    half = wm.shape[-1] // 2
    return jnp.concatenate([-wm[..., half:], wm[..., :half]], axis=-1)


def _layer_weights(l, w_in, sb_norm, conv_w, conv_b, dt_bias, a_log, d_skip, ssd_norm, q_norm, w_uq, kv_norm,
                   w_uk, w_uv, mla_norm, w_out, ln1_g, ln1_b, w_up, w_down, ln2_g, ln2_b):
    wi = w_in[l]
    o = [0, 256, 384, 512, 1024, 2048, 2056, 2312, 2568, 2600]
    sq, sk, sv, z, xbc, dtc, cq, ckv, kpe = [wi[:, o[i]:o[i + 1]] for i in range(9)]
    d = wi.shape[0]
    dt_pad = jnp.concatenate([dtc, jnp.zeros((d, LANE - SSD_HEADS), F32)], axis=1)
    kpe_grp = jnp.concatenate([kpe, _rot_cols(kpe), jnp.zeros((d, LANE - 2 * MLA_ROPE), F32)], axis=1)
    w1 = jnp.concatenate([sq, sk, sv, z, xbc, cq, ckv, dt_pad, kpe_grp], axis=1).astype(BF16)

    uq = w_uq[l]
    zq = lambda width: jnp.zeros((MLA_QR, MLA_HEADS, width), F32)
    pe = uq[..., MLA_NOPE:]
    w2a = jnp.concatenate([uq[..., :MLA_NOPE], pe, zq(HEAD_PAD - MLA_NOPE - MLA_ROPE)], axis=-1)
    w2b = jnp.concatenate([zq(MLA_NOPE), _rot_cols(pe), zq(HEAD_PAD - MLA_NOPE - MLA_ROPE)], axis=-1)
    w2 = jnp.concatenate([w2a.reshape(MLA_QR, -1), w2b.reshape(MLA_QR, -1)], axis=1).astype(BF16)

    uk = w_uk[l]
    wk_prompt = jnp.concatenate([uk, jnp.zeros((MLA_KVR, MLA_HEADS, HEAD_PAD - MLA_NOPE), F32)], axis=-1)
    wk_prompt = wk_prompt.reshape(MLA_KVR, MLA_HEADS * HEAD_PAD).astype(BF16)
    ukt = jnp.transpose(uk, (1, 2, 0))
    wk_decode = jnp.concatenate([ukt, jnp.zeros((MLA_HEADS, HEAD_PAD - MLA_NOPE, MLA_KVR), F32)], axis=1)
    wk_decode = wk_decode.reshape(MLA_HEADS * HEAD_PAD, MLA_KVR).astype(BF16)
    wv = w_uv[l].reshape(MLA_KVR, MLA_W).astype(BF16)

    pad_heads = lambda v: jnp.concatenate([v, jnp.zeros((LANE - SSD_HEADS,), F32)])[None, :]
    return {
        "w1": w1, "w2": w2, "wk_prompt": wk_prompt, "wk_decode": wk_decode, "wv": wv, "wv_t": wv.T,
        "q_norm": q_norm[l][None, :], "kv_norm": kv_norm[l][None, :],
        "conv_w": conv_w[l], "conv_b": conv_b[l][None, :],
        "dt_bias": pad_heads(dt_bias[l]), "a_log": pad_heads(a_log[l]),
        "d_skip_e": jnp.repeat(d_skip[l], SSD_P)[None, :], "ssd_norm": ssd_norm[l][None, :],
        "sb_norm": sb_norm[l][None, :], "mla_norm": mla_norm[l][None, :],
        "w_out": w_out[l].astype(BF16), "ln1_g": ln1_g[l][None, :], "ln1_b": ln1_b[l][None, :],
        "w_up": w_up[l].astype(BF16), "w_down": w_down[l].astype(BF16),
        "ln2_g": ln2_g[l][None, :], "ln2_b": ln2_b[l][None, :],
    }


def _rope_table(pos):
    half = MLA_ROPE // 2
    inv = ROPE_THETA ** (-jnp.arange(half, dtype=F32) / half)
    ang = pos.astype(F32)[:, None] * inv[None, :]
    cos, sin = jnp.cos(ang), jnp.sin(ang)
    pad = jnp.zeros((pos.shape[0], LANE - 2 * MLA_ROPE), F32)
    return jnp.concatenate([cos, cos, sin, sin, pad], axis=1)


def _pick(n, candidates):
    for c in candidates:
        if n % c == 0:
            return c
    return n


def kernel(x_prompt, x_sample, cache_sb_k, cache_sb_v, cache_mla_ckv, cache_mla_kpe, state_ssm, state_conv,
           page_table, w_in, sb_norm, conv_w, conv_b, dt_bias, a_log, d_skip, ssd_norm, q_norm, w_uq, kv_norm,
           w_uk, w_uv, mla_norm, w_out, ln1_g, ln1_b, w_up, w_down, ln2_g, ln2_b):
    bp, seq, d = x_prompt.shape
    nd, dec_seq, _ = x_sample.shape
    assert dec_seq == 1 and d == D_MODEL
    depth = w_in.shape[0]
    n_pages = page_table.shape[1]
    past_len = n_pages * PAGE
    n_pool = cache_sb_k.shape[1]

    tab_p = _rope_table(jnp.arange(seq, dtype=jnp.int32))
    tab_s = _rope_table(jnp.full((nd,), past_len, dtype=jnp.int32))
    kc = jnp.transpose(cache_sb_k, (0, 1, 3, 4, 2)).reshape(depth, n_pool, SB_KV * SB_DIM, PAGE)
    vc = jnp.transpose(cache_sb_v, (0, 1, 3, 4, 2)).reshape(depth, n_pool, SB_KV * SB_DIM, PAGE)
    pc = jnp.transpose(cache_mla_kpe, (0, 1, 3, 2))
    conv_all = jnp.transpose(state_conv, (0, 2, 1, 3)).reshape(depth * (SSD_CONV - 1), nd, SSD_CH)
    h0_all = state_ssm.reshape(depth * nd, SSD_INNER, SSD_N)

    tm_p = _pick(seq, (256, 128))
    tq = _pick(seq, (256, 128))
    chunk = _pick(seq, (128,))
    tq_mla = _pick(seq, (2 * tm_p, tm_p))

    hp = x_prompt.reshape(bp * seq, d)
    hs = x_sample.reshape(nd, d)
    new_p, new_s = [], []
    for l in range(depth):
        w = _layer_weights(l, w_in, sb_norm, conv_w, conv_b, dt_bias, a_log, d_skip, ssd_norm, q_norm, w_uq,
                           kv_norm, w_uk, w_uv, mla_norm, w_out, ln1_g, ln1_b, w_up, w_down, ln2_g, ln2_b)
        wp = dict(w, wk=w["wk_prompt"], wv=w["wv_t"])
        (sq, sk, sv, z, xbc, dt, ckv, kpe, kb, vb, qh, kh, vmt) = _proj_call(hp, tab_p, wp, False, tm_p)
        sb_o = _sb_prompt_call(sq, kb, vb, bp, seq, tq)
        mla_o = _mla_prompt_call(qh, kh, vmt, bp, seq, tq_mla, tm_p)
        ssd_o, ssm_new, conv_new = _ssd_prompt_call(z, xbc, dt, w, bp, seq, chunk)
        hp = _out_call(hp, sb_o, ssd_o, mla_o, w, tm_p)
        new_p.append((sk.reshape(bp, seq, SB_KV, SB_DIM), sv.reshape(bp, seq, SB_KV, SB_DIM),
                      ckv.reshape(bp, seq, MLA_KVR), kpe.reshape(bp, seq, MLA_ROPE),
                      ssm_new.reshape(bp, SSD_HEADS, SSD_P, SSD_N), conv_new))
        wd = dict(w, wk=w["wk_decode"])
        (sq, sk, sv, z, xbc, dt, ckv, kpe, qh, qlat) = _proj_call(hs, tab_s, wd, True, nd)
        q4 = sq.reshape(nd, SB_HEADS, SB_DIM)
        zq = jnp.zeros_like(q4)
        grp = (jnp.arange(SB_HEADS) // (SB_HEADS // SB_KV))[None, :, None]
        qbd = jnp.concatenate([jnp.where(grp == 0, q4, zq), jnp.where(grp == 1, q4, zq)], axis=-1)
        qbd = jnp.concatenate([qbd, jnp.zeros_like(qbd)], axis=1)
        sb_o = _sb_decode_call(page_table, qbd, kc, vc, l).reshape(nd, SB_W)
        pad8 = lambda a: jnp.concatenate([a, jnp.zeros_like(a)], axis=1)
        qlat8 = pad8(qlat.reshape(nd, MLA_HEADS, MLA_KVR))
        qpe8 = pad8(qh.reshape(nd, MLA_HEADS, HEAD_PAD)[:, :, MLA_NOPE:MLA_NOPE + MLA_ROPE])
        mla_o = _mla_decode_call(page_table, qlat8, qpe8, ckv[:, None, :], kpe[:, None, :], w["wv"],
                                 cache_mla_ckv, pc, l).reshape(nd, MLA_W)
        ssd_o, ssm_new, conv_new_t = _ssd_decode_call(z, xbc, dt, conv_all, h0_all, w, l)
        hs = _out_call(hs, sb_o, ssd_o, mla_o, w, nd)
        new_s.append((sk.reshape(nd, 1, SB_KV, SB_DIM), sv.reshape(nd, 1, SB_KV, SB_DIM),
                      ckv.reshape(nd, 1, MLA_KVR), kpe.reshape(nd, 1, MLA_ROPE),
                      ssm_new.reshape(nd, SSD_HEADS, SSD_P, SSD_N), jnp.transpose(conv_new_t, (1, 0, 2))))

    stack = lambda states, i: jnp.stack([st[i] for st in states])
    return (hp.reshape(bp, seq, d), hs.reshape(nd, 1, d),
            *[stack(new_p, i) for i in range(6)], *[stack(new_s, i) for i in range(6)])
```

```python
import functools
import math

import jax
import jax.numpy as jnp
from jax import lax
from jax.experimental import pallas as pl
from jax.experimental.pallas import tpu as pltpu

F32 = jnp.float32
BF16 = jnp.bfloat16

D_MODEL = 1024
PAGE = 128
SB_HEADS = 4
SB_KV = 2
SB_DIM = 64
SB_W = SB_HEADS * SB_DIM
SSD_HEADS = 8
SSD_P = 64
SSD_INNER = SSD_HEADS * SSD_P
SSD_GROUPS = 2
SSD_N = 128
SSD_CONV = 4
SSD_CH = SSD_INNER + 2 * SSD_GROUPS * SSD_N
MLA_HEADS = 4
MLA_NOPE = 64
MLA_ROPE = 32
MLA_V = 64
MLA_QR = 256
MLA_KVR = 256
MLA_W = MLA_HEADS * MLA_V
MLA_SCALE = (MLA_NOPE + MLA_ROPE) ** -0.5
LOG2_E = math.log2(math.e)
ROPE_THETA = 10000.0
D_FF = 4 * D_MODEL
EPS = 1e-5
DEPTH = 2
ALPHA = (2 * DEPTH) ** 0.25
LANE = 128
HEAD_PAD = 128

SB_DEAD = -104.0

VMEM_LIMIT = 56 * 1024 * 1024

C_SQ, C_SK, C_SV, C_Z, C_XBC, C_CQ, C_CKV, C_DT, C_KPE, C_END = (
    0, 256, 384, 512, 1024, 2048, 2304, 2560, 2688, 2816)


def _dot(a, b):
    return jnp.dot(a, b, preferred_element_type=F32)


def _dot_nt(a, b):
    return lax.dot_general(a, b, (((1,), (1,)), ((), ())), preferred_element_type=F32)


def _dot_tn(a, b):
    return lax.dot_general(a, b, (((0,), (0,)), ((), ())), preferred_element_type=F32)


def _split_hi_lo(x):
    hi = x.astype(BF16)
    lo = (x - hi.astype(F32)).astype(BF16)
    return hi, lo


def _dot_f32ish(x, w01):
    hi, lo = _split_hi_lo(x)
    return _dot(hi, w01) + _dot(lo, w01)


def _rms(x, g):
    return x * lax.rsqrt(jnp.mean(x * x, axis=-1, keepdims=True) + EPS) * g


def _layernorm(x, g, b):
    mu = jnp.mean(x, axis=-1, keepdims=True)
    xc = x - mu
    var = jnp.mean(xc * xc, axis=-1, keepdims=True)
    return xc * lax.rsqrt(var + EPS) * g + b


def _softplus(x):
    return jnp.maximum(x, 0.0) + jnp.log1p(jnp.exp(-jnp.abs(x)))


def _silu(x):
    return x / (1.0 + jnp.exp(-x))


def _full(shape):
    nd = len(shape)
    return pl.BlockSpec(shape, lambda *_: (0,) * nd)


def _params(sem):
    return pltpu.CompilerParams(dimension_semantics=sem, vmem_limit_bytes=VMEM_LIMIT)


def _proj_body(decode, x_ref, tab_ref, w1_ref, w2_ref, wk_ref, wv_ref, qg_ref, kvg_ref, *outs):
    if decode:
        (sq_ref, sk_ref, sv_ref, z_ref, xbc_ref, dt_ref, ckv_ref, kpe_ref, qh_ref, qlat_ref) = outs
    else:
        (sq_ref, sk_ref, sv_ref, z_ref, xbc_ref, dt_ref, ckv_ref, kpe_ref,
         kb_ref, vb_ref, qh_ref, kh_ref, vm_ref) = outs
    x = x_ref[...].astype(BF16)

    def col(lo, hi):
        return _dot(x, w1_ref[:, lo:hi])

    sq_ref[...] = col(C_SQ, C_SK)
    sk = col(C_SK, C_SV)
    sv = col(C_SV, C_Z)
    sk_ref[...] = sk
    sv_ref[...] = sv
    z_ref[...] = col(C_Z, C_XBC)
    xbc_ref[...] = col(C_XBC, C_CQ)
    dt_ref[...] = col(C_DT, C_KPE)

    tab = tab_ref[...]
    lane = lax.broadcasted_iota(jnp.int32, tab.shape, 1)
    rope_q = jnp.logical_and(lane >= MLA_NOPE, lane < MLA_NOPE + MLA_ROPE)
    cosq = jnp.where(lane < MLA_NOPE, 1.0, jnp.where(rope_q, pltpu.roll(tab, 64, 1), 0.0))
    sinq = jnp.where(rope_q, pltpu.roll(tab, 32, 1), 0.0)
    cosk = jnp.where(lane < MLA_ROPE, tab, 0.0)
    sink = jnp.where(lane < MLA_ROPE, pltpu.roll(tab, 96, 1), 0.0)

    cqn = _rms(col(C_CQ, C_CKV), qg_ref[...]).astype(BF16)
    qa = _dot(cqn, w2_ref[:, 0:512])
    qb = _dot(cqn, w2_ref[:, 512:1024])
    q_scale = MLA_SCALE if decode else MLA_SCALE * LOG2_E
    qh = []
    for h in range(MLA_HEADS):
        sl = slice(h * HEAD_PAD, (h + 1) * HEAD_PAD)
        qh.append((qa[:, sl] * cosq + qb[:, sl] * sinq) * q_scale)

    ckvn = _rms(col(C_CKV, C_DT), kvg_ref[...])
    ckv_ref[...] = ckvn
    kg = col(C_KPE, C_END)
    kpe = kg * cosk + pltpu.roll(kg, 96, 1) * sink
    kpe_ref[...] = kpe[:, 0:MLA_ROPE]

    if decode:
        for h in range(MLA_HEADS):
            sl = slice(h * HEAD_PAD, (h + 1) * HEAD_PAD)
            qh_ref[:, sl] = qh[h]
            qlat_ref[:, h * MLA_KVR:(h + 1) * MLA_KVR] = _dot(qh[h].astype(BF16), wk_ref[sl, :])
    else:
        kb_ref[...] = sk.astype(BF16)
        vb_ref[...] = sv.astype(BF16)
        ckvb = ckvn.astype(BF16)
        kpe_at64 = pltpu.roll(kpe, 64, 1)
        for h in range(MLA_HEADS):
            sl = slice(h * HEAD_PAD, (h + 1) * HEAD_PAD)
            qh_ref[:, sl] = qh[h].astype(BF16)
            kh_ref[:, sl] = (_dot(ckvb, wk_ref[:, sl]) + kpe_at64).astype(BF16)
        tk = vm_ref.shape[2]
        for j in range(vm_ref.shape[0]):
            vm_ref[j] = _dot_nt(wv_ref[...], ckvb[j * tk:(j + 1) * tk, :]).astype(BF16)


def _proj_call(x, tab, w, decode, tm, tk=None):
    n = x.shape[0]
    nt = tab.shape[0] // tm
    grid = (n // tm,)
    row = lambda width: pl.BlockSpec((tm, width), lambda i: (i, 0))
    in_specs = [
        row(D_MODEL),
        pl.BlockSpec((tm, LANE), lambda i: (i % nt, 0)),
        _full(w["w1"].shape), _full(w["w2"].shape), _full(w["wk"].shape), _full(w["wv"].shape),
        _full((1, MLA_QR)), _full((1, MLA_KVR)),
    ]
    f = lambda width, dt=F32: jax.ShapeDtypeStruct((n, width), dt)
    out_shape = [f(256), f(128), f(128), f(512), f(1024), f(128), f(256), f(MLA_ROPE)]
    out_specs = [row(256), row(128), row(128), row(512), row(1024), row(128), row(256), row(MLA_ROPE)]
    if decode:
        out_shape += [f(512), f(MLA_HEADS * MLA_KVR)]
        out_specs += [row(512), row(MLA_HEADS * MLA_KVR)]
    else:
        out_shape += [f(128, BF16), f(128, BF16), f(512, BF16), f(512, BF16),
                      jax.ShapeDtypeStruct((n // tk, MLA_W, tk), BF16)]
        out_specs += [row(128), row(128), row(512), row(512),
                      pl.BlockSpec((tm // tk, MLA_W, tk), lambda i: (i, 0, 0))]
    return pl.pallas_call(
        functools.partial(_proj_body, decode),
        grid=grid, in_specs=in_specs, out_specs=out_specs, out_shape=out_shape,
        compiler_params=_params(("arbitrary",)),
        name="proj_decode" if decode else "proj_prompt",
    )(x, tab, w["w1"], w["w2"], w["wk"], w["wv"], w["q_norm"], w["kv_norm"])


def _suffix_matrix(tk):
    j = lax.broadcasted_iota(jnp.int32, (tk, tk), 0)
    s = lax.broadcasted_iota(jnp.int32, (tk, tk), 1)
    return jnp.where(j > s, 1.0, 0.0).astype(BF16)


def _sb_block(qall, kblk, vblk, umat, c, mask, keys_on_lanes=False):
    z = (_dot(qall, kblk) if keys_on_lanes else _dot_nt(qall, kblk)) * (SB_DIM ** -0.5)
    lse = jnp.log1p(jnp.exp(-jnp.abs(z)))
    log_beta = jnp.minimum(z, 0.0) - lse
    log_keep = log_beta - z
    if mask is not None:
        log_keep = jnp.where(mask, log_keep, 0.0)
    hi, lo = _split_hi_lo(log_keep)
    after = _dot(hi, umat) + _dot(lo, umat)
    w = jnp.exp(log_beta + after + c)
    if mask is not None:
        w = jnp.where(mask, w, 0.0)
    wb = w.astype(BF16)
    pv = _dot_nt(wb, vblk) if keys_on_lanes else _dot(wb, vblk)
    return pv, c + jnp.sum(log_keep, axis=1, keepdims=True)


def _sb_prompt_body(q_ref, k_ref, v_ref, o_ref, qall_ref, acc_ref, c_ref, *, tq):
    i = pl.program_id(1)
    tk = tq
    lane = lax.broadcasted_iota(jnp.int32, (tq, LANE), 1)
    lo_half = lane < SB_DIM
    s0 = q_ref[:, 0:128]
    s1 = q_ref[:, 128:256]
    qall_ref[0 * tq:1 * tq, :] = jnp.where(lo_half, s0, 0.0).astype(BF16)
    qall_ref[1 * tq:2 * tq, :] = jnp.where(lo_half, pltpu.roll(s0, 64, 1), 0.0).astype(BF16)
    qall_ref[2 * tq:3 * tq, :] = jnp.where(lo_half, 0.0, pltpu.roll(s1, 64, 1)).astype(BF16)
    qall_ref[3 * tq:4 * tq, :] = jnp.where(lo_half, 0.0, s1).astype(BF16)
    umat = _suffix_matrix(tk)

    row = lax.rem(lax.broadcasted_iota(jnp.int32, (SB_HEADS * tq, tk), 0), tq)
    colm = lax.broadcasted_iota(jnp.int32, (SB_HEADS * tq, tk), 1)
    mask = colm < row

    def run(kb, c, m):
        start = pl.multiple_of(kb * tk, tk)
        return _sb_block(qall_ref[...], k_ref[pl.ds(start, tk), :], v_ref[pl.ds(start, tk), :], umat, c, m)

    pv, c = run(i, jnp.zeros((SB_HEADS * tq, 1), F32), mask)
    acc_ref[...] = pv
    c_ref[...] = c

    def cond(st):
        kb, cmax = st
        return jnp.logical_and(kb >= 0, cmax > SB_DEAD)

    def body(st):
        kb, _ = st
        pv, c = run(kb, c_ref[...], None)
        acc_ref[...] += pv
        c_ref[...] = c
        return kb - 1, jnp.max(c)

    lax.while_loop(cond, body, (i - 1, jnp.max(c)))

    o0 = acc_ref[0 * tq:1 * tq, :]
    o1 = acc_ref[1 * tq:2 * tq, :]
    o2 = acc_ref[2 * tq:3 * tq, :]
    o3 = acc_ref[3 * tq:4 * tq, :]
    o_ref[:, 0:128] = jnp.where(lo_half, o0, pltpu.roll(o1, 64, 1))
    o_ref[:, 128:256] = jnp.where(lo_half, pltpu.roll(o2, 64, 1), o3)


def _sb_prompt_call(sq, kb, vb, bsz, seq, tq):
    nq = seq // tq
    return pl.pallas_call(
        functools.partial(_sb_prompt_body, tq=tq),
        grid=(bsz, nq),
        in_specs=[
            pl.BlockSpec((tq, SB_W), lambda b, i: (b * nq + i, 0)),
            pl.BlockSpec((seq, 128), lambda b, i: (b, 0)),
            pl.BlockSpec((seq, 128), lambda b, i: (b, 0)),
        ],
        out_specs=pl.BlockSpec((tq, SB_W), lambda b, i: (b * nq + i, 0)),
        out_shape=jax.ShapeDtypeStruct((bsz * seq, SB_W), F32),
        scratch_shapes=[
            pltpu.VMEM((SB_HEADS * tq, LANE), BF16),
            pltpu.VMEM((SB_HEADS * tq, LANE), F32),
            pltpu.VMEM((SB_HEADS * tq, 1), F32),
        ],
        compiler_params=_params(("arbitrary", "arbitrary")),
        name="sb_prompt",
    )(sq, kb, vb)


SB_DEC_PAGES = 2


def _sb_decode_body(pt_ref, q_ref, kc_ref, vc_ref, o_ref, kbuf, vbuf, sem, *, layer, n_pages):
    s = pl.program_id(0)
    g_pages = SB_DEC_PAGES
    ng = n_pages // g_pages
    tk = g_pages * PAGE

    def copies(seq, g, slot):
        out = []
        for j in range(g_pages):
            page = pt_ref[seq, g * g_pages + j]
            dst = pl.ds(j * PAGE, PAGE)
            out.append(pltpu.make_async_copy(kc_ref.at[layer, page], kbuf.at[slot, :, dst], sem.at[0, slot]))
            out.append(pltpu.make_async_copy(vc_ref.at[layer, page], vbuf.at[slot, :, dst], sem.at[1, slot]))
        return out

    def start(seq, g, slot):
        for cp in copies(seq, g, slot):
            cp.start()

    def wait(seq, g, slot):
        for cp in copies(seq, g, slot):
            cp.wait()

    newest_slot = lambda seq: 2 + lax.rem(seq, 2)
    slot_of = lambda g: lax.rem(ng - 2 - g, 2)
    umat = _suffix_matrix(tk)
    qall = q_ref[0].astype(BF16)

    def step(slot, c, acc):
        pv, c = _sb_block(qall, kbuf[slot].astype(BF16), vbuf[slot].astype(BF16), umat, c, None,
                          keys_on_lanes=True)
        return jnp.max(c[0:SB_HEADS]), c, acc + pv

    @pl.when(s == 0)
    def _():
        start(s, ng - 1, newest_slot(s))

    @pl.when(s + 1 < pl.num_programs(0))
    def _():
        start(s + 1, ng - 1, newest_slot(s + 1))

    if ng >= 2:
        start(s, ng - 2, 0)
    wait(s, ng - 1, newest_slot(s))
    first = step(newest_slot(s), jnp.zeros((8, 1), F32), jnp.zeros((8, LANE), F32))

    def cond(st):
        g, cmax, _, _ = st
        return jnp.logical_and(g >= 0, cmax > SB_DEAD)

    def body(st):
        g, _, c, acc = st
        slot = slot_of(g)

        @pl.when(g >= 1)
        def _():
            start(s, g - 1, 1 - slot)

        wait(s, g, slot)
        return (g - 1, *step(slot, c, acc))

    g_end, _, _, acc = lax.while_loop(cond, body, (jnp.int32(ng - 2), *first))

    @pl.when(g_end >= 0)
    def _():
        wait(s, g_end, slot_of(g_end))

    lane = lax.broadcasted_iota(jnp.int32, (1, LANE), 1)
    lo_half = lane < SB_DIM
    o_ref[0, :, 0:128] = jnp.where(lo_half, acc[0:1], pltpu.roll(acc[1:2], 64, 1))
    o_ref[0, :, 128:256] = jnp.where(lo_half, pltpu.roll(acc[2:3], 64, 1), acc[3:4])


def _sb_decode_call(page_table, qbd, kc, vc, layer):
    n, n_pages = page_table.shape
    grid_spec = pltpu.PrefetchScalarGridSpec(
        num_scalar_prefetch=1,
        grid=(n,),
        in_specs=[
            pl.BlockSpec((1, 8, LANE), lambda s, pt: (s, 0, 0)),
            pl.BlockSpec(memory_space=pl.ANY),
            pl.BlockSpec(memory_space=pl.ANY),
        ],
        out_specs=pl.BlockSpec((1, 1, SB_W), lambda s, pt: (s, 0, 0)),
        scratch_shapes=[
            pltpu.VMEM((4, LANE, SB_DEC_PAGES * PAGE), F32),
            pltpu.VMEM((4, LANE, SB_DEC_PAGES * PAGE), F32),
            pltpu.SemaphoreType.DMA((2, 4)),
        ],
    )
    return pl.pallas_call(
        functools.partial(_sb_decode_body, layer=layer, n_pages=n_pages),
        grid_spec=grid_spec,
        out_shape=jax.ShapeDtypeStruct((n, 1, SB_W), F32),
        compiler_params=_params(("arbitrary",)),
        name="sb_decode",
    )(page_table, qbd, kc, vc)


def _mla_prompt_body(q_ref, k_ref, vt_ref, o_ref, *, tq, tk):
    i = pl.program_id(1)
    ratio = tq // tk
    key = lax.broadcasted_iota(jnp.int32, (tk, tq), 0)
    qry = lax.broadcasted_iota(jnp.int32, (tk, tq), 1)

    heads = range(MLA_HEADS)

    def scores(kb):
        start = pl.multiple_of(kb * tk, tk)
        return tuple(_dot_nt(k_ref[pl.ds(start, tk), h * HEAD_PAD:(h + 1) * HEAD_PAD],
                             q_ref[:, h * HEAD_PAD:(h + 1) * HEAD_PAD]) for h in heads)

    def absorb(kb, sts, state, mask):
        ps, ms, ls, alphas = [], [], [], []
        for h in heads:
            m_old, l_old, _ = state[h]
            st = sts[h] if mask is None else jnp.where(mask, sts[h], -jnp.inf)
            m_new = jnp.maximum(m_old, jnp.max(st, axis=0, keepdims=True))
            p = jnp.exp2(st - m_new)
            alpha = jnp.exp2(m_old - m_new)
            ls.append(alpha * l_old + jnp.sum(p, axis=0, keepdims=True))
            ms.append(m_new)
            alphas.append(alpha)
            ps.append(p.astype(BF16))
        pvs = [_dot(vt_ref[kb, h * MLA_V:(h + 1) * MLA_V, :], ps[h]) for h in heads]
        return tuple((ms[h], ls[h], alphas[h] * state[h][2] + pvs[h]) for h in heads)

    init = tuple((jnp.full((1, tq), -jnp.inf, F32), jnp.zeros((1, tq), F32), jnp.zeros((MLA_V, tq), F32))
                 for _ in range(MLA_HEADS))
    n_full = i * ratio

    def trip(kb0, state, masks):
        sts = [scores(kb0 + j) for j in range(ratio)]
        for j in range(ratio):
            state = absorb(kb0 + j, sts[j], state, masks[j])
        return state

    state = lax.fori_loop(0, i, lambda t, st: trip(t * ratio, st, [None] * ratio), init)
    state = trip(n_full, state, [key + j * tk <= qry for j in range(ratio)])
    o_t = jnp.concatenate([acc / l for (_, l, acc) in state], axis=0)
    o_ref[...] = o_t.T


def _mla_prompt_call(qh, kh, vmt, bsz, seq, tq, tk):
    nq = seq // tq
    nk = seq // tk
    return pl.pallas_call(
        functools.partial(_mla_prompt_body, tq=tq, tk=tk),
        grid=(bsz, nq),
        in_specs=[
            pl.BlockSpec((tq, 512), lambda b, i: (b * nq + i, 0)),
            pl.BlockSpec((seq, 512), lambda b, i: (b, 0)),
            pl.BlockSpec((nk, MLA_W, tk), lambda b, i: (b, 0, 0)),
        ],
        out_specs=pl.BlockSpec((tq, MLA_W), lambda b, i: (b * nq + i, 0)),
        out_shape=jax.ShapeDtypeStruct((bsz * seq, MLA_W), F32),
        compiler_params=_params(("arbitrary", "arbitrary")),
        name="mla_prompt",
    )(qh, kh, vmt)


MLA_DEC_PAGES = 32
MLA_DEC_BUFS = 4


def _mla_decode_body(pt_ref, ql_ref, qp_ref, cn_ref, pn_ref, wuv_ref, cc_ref, pc_ref, o_ref,
                     cbuf, pbuf, cb16, sc_keep, sem, *, layer, n_pages):
    s = pl.program_id(0)
    n = pl.num_programs(0)
    g_pages = MLA_DEC_PAGES
    ng = n_pages // g_pages
    nbuf = MLA_DEC_BUFS
    total = n * ng

    def copies(t):
        seq_raw = lax.div(t, jnp.int32(ng))
        g = t - seq_raw * ng
        seq = lax.rem(seq_raw, n)
        slot = lax.rem(t, jnp.int32(nbuf))
        out = []
        for j in range(g_pages):
            page = pt_ref[seq, g * g_pages + j]
            dst = pl.ds(j * PAGE, PAGE)
            out.append(pltpu.make_async_copy(cc_ref.at[layer, page], cbuf.at[slot, dst], sem.at[0, slot]))
            out.append(pltpu.make_async_copy(pc_ref.at[layer, page], pbuf.at[slot, :, dst], sem.at[1, slot]))
        return out

    def start(t):
        for cp in copies(t):
            cp.start()

    def wait(t):
        for cp in copies(t):
            cp.wait()

    def scores(t, seq):
        wait(t)
        slot = lax.rem(t, jnp.int32(nbuf))
        cb = cbuf[slot].astype(BF16)
        cb16[lax.rem(t, jnp.int32(2))] = cb
        return (_dot_nt(ql_ref[seq].astype(BF16), cb)
                + _dot(qp_ref[seq].astype(BF16), pbuf[slot].astype(BF16)))

    def refill(t):
        @pl.when(t + nbuf <= total)
        def _():
            start(t + nbuf)

    def absorb(t, sc, st):
        m_old, l_old, acc = st
        m_new = jnp.maximum(m_old, jnp.max(sc, axis=1, keepdims=True))
        p = jnp.exp(sc - m_new)
        alpha = jnp.exp(m_old - m_new)
        pv = _dot(p.astype(BF16), cb16[lax.rem(t, jnp.int32(2))])
        return m_new, alpha * l_old + jnp.sum(p, axis=1, keepdims=True), alpha * acc + pv

    @pl.when(s == 0)
    def _():
        for t0 in range(nbuf):
            start(t0)
        sc_keep[...] = scores(0, 0)

    t_first = s * ng

    def body(g, carry):
        sc, st = carry
        t = t_first + g
        sc_next = scores(t + 1, s)
        st = absorb(t, sc, st)
        refill(t)
        return sc_next, st

    init = (jnp.full((8, 1), -jnp.inf, F32), jnp.zeros((8, 1), F32), jnp.zeros((8, MLA_KVR), F32))
    sc, st = lax.fori_loop(0, ng - 1, body, (sc_keep[...], init))
    t_last = t_first + ng - 1
    sc_keep[...] = scores(t_last + 1, lax.rem(s + 1, n))
    m_old, l_old, acc = absorb(t_last, sc, st)
    refill(t_last)
    ql = ql_ref[s]
    qp = qp_ref[s]

    cn = cn_ref[0]
    pn = pn_ref[0]
    sc = jnp.sum(ql * cn, axis=1, keepdims=True) + jnp.sum(qp * pn, axis=1, keepdims=True)
    m_new = jnp.maximum(m_old, sc)
    p = jnp.exp(sc - m_new)
    alpha = jnp.exp(m_old - m_new)
    l_new = alpha * l_old + p
    o_lat = (alpha * acc + p * cn) / l_new
    o_all = _dot(o_lat.astype(BF16), wuv_ref[...])
    sub = lax.broadcasted_iota(jnp.int32, (8, MLA_W), 0)
    lane = lax.broadcasted_iota(jnp.int32, (8, MLA_W), 1)
    pick = (lane // MLA_V) == sub
    o_ref[0] = jnp.sum(jnp.where(pick, o_all, 0.0), axis=0, keepdims=True)


def _mla_decode_call(page_table, qlat, qpe, ckv_new, kpe_new, wuv, cc, pc, layer):
    n, n_pages = page_table.shape
    tk = MLA_DEC_PAGES * PAGE
    grid_spec = pltpu.PrefetchScalarGridSpec(
        num_scalar_prefetch=1,
        grid=(n,),
        in_specs=[
            pl.BlockSpec((n, 8, MLA_KVR), lambda s, pt: (0, 0, 0)),
            pl.BlockSpec((n, 8, MLA_ROPE), lambda s, pt: (0, 0, 0)),
            pl.BlockSpec((1, 1, MLA_KVR), lambda s, pt: (s, 0, 0)),
            pl.BlockSpec((1, 1, MLA_ROPE), lambda s, pt: (s, 0, 0)),
            pl.BlockSpec((MLA_KVR, MLA_W), lambda s, pt: (0, 0)),
            pl.BlockSpec(memory_space=pl.ANY),
            pl.BlockSpec(memory_space=pl.ANY),
        ],
        out_specs=pl.BlockSpec((1, 1, MLA_W), lambda s, pt: (s, 0, 0)),
        scratch_shapes=[
            pltpu.VMEM((MLA_DEC_BUFS, tk, MLA_KVR), F32),
            pltpu.VMEM((MLA_DEC_BUFS, MLA_ROPE, tk), F32),
            pltpu.VMEM((2, tk, MLA_KVR), BF16),
            pltpu.VMEM((8, tk), F32),
            pltpu.SemaphoreType.DMA((2, MLA_DEC_BUFS)),
        ],
    )
    assert n_pages % MLA_DEC_PAGES == 0 and n * (n_pages // MLA_DEC_PAGES) >= MLA_DEC_BUFS
    return pl.pallas_call(
        functools.partial(_mla_decode_body, layer=layer, n_pages=n_pages),
        grid_spec=grid_spec,
        out_shape=jax.ShapeDtypeStruct((n, 1, MLA_W), F32),
        compiler_params=_params(("arbitrary",)),
        name="mla_decode",
    )(page_table, qlat, qpe, ckv_new, kpe_new, wuv, cc, pc)


def _expand_matrix():
    r = lax.broadcasted_iota(jnp.int32, (LANE, SSD_INNER), 0)
    c = lax.broadcasted_iota(jnp.int32, (LANE, SSD_INNER), 1)
    return jnp.where(c // SSD_P == r, 1.0, 0.0).astype(BF16)


def _gated_norm(y, z, norm_w):
    g = y * _silu(z)
    half = SSD_INNER // SSD_GROUPS
    parts = []
    for k in range(SSD_GROUPS):
        gk = g[:, k * half:(k + 1) * half]
        parts.append(gk * lax.rsqrt(jnp.mean(gk * gk, axis=-1, keepdims=True) + EPS))
    return jnp.concatenate(parts, axis=1) * norm_w


def _ssd_prompt_body(z_ref, xbc_ref, dt_ref, cw_ref, cb_ref, dtb_ref, alog_ref, dskip_ref, nw_ref,
                     y_ref, hout_ref, cout_ref, xwin, ht, *, chunk):
    c = pl.program_id(1)
    nc = pl.num_programs(1)
    L = chunk

    @pl.when(c == 0)
    def _():
        xwin[0:8, :] = jnp.zeros((8, SSD_CH), F32)
        ht[...] = jnp.zeros(ht.shape, F32)

    xwin[8:8 + L, :] = xbc_ref[...]
    conv = cb_ref[...] + xwin[5:5 + L, :] * cw_ref[0:1, :]
    for i in range(1, SSD_CONV):
        conv = conv + xwin[5 + i:5 + i + L, :] * cw_ref[i:i + 1, :]
    xwin[0:8, :] = xwin[L:L + 8, :]
    cv = _silu(conv)
    xs = cv[:, 0:SSD_INNER]
    bm = cv[:, SSD_INNER:SSD_INNER + 256]
    cm = cv[:, SSD_INNER + 256:SSD_INNER + 512]

    dt = _softplus(dt_ref[...] + dtb_ref[...])
    da = dt * (-jnp.exp(alog_ref[...]))
    r = lax.broadcasted_iota(jnp.int32, (L, L), 0)
    s = lax.broadcasted_iota(jnp.int32, (L, L), 1)
    causal = s <= r
    tri = jnp.where(causal, 1.0, 0.0).astype(BF16)
    cs = _dot_f32ish_left(tri, da)
    cs_t = cs.T
    cs_last = cs[L - 1:L, :]
    ecs = jnp.exp(cs)
    emat = _expand_matrix()
    dt_e = _dot_f32ish(dt, emat)
    ecs_e = _dot_f32ish(ecs, emat)
    te_e = _dot_f32ish(dt * jnp.exp(cs_last - cs), emat)
    last_e = _dot_f32ish(jnp.exp(cs_last), emat)

    x_dt = (xs * dt_e).astype(BF16)
    x_end = (xs * te_e).astype(BF16)
    lane = lax.broadcasted_iota(jnp.int32, (L, LANE), 1)
    lo_half = lane < SSD_P
    y_parts = []
    for g in range(SSD_GROUPS):
        gs = slice(g * SSD_N, (g + 1) * SSD_N)
        cg = cm[:, gs].astype(BF16)
        bg = bm[:, gs].astype(BF16)
        cb = _dot_nt(cg, bg)
        hs = slice(g * 256, (g + 1) * 256)
        h_in = ht[g]
        y_off = _dot(cg, h_in.astype(BF16)) * ecs_e[:, hs]
        yd = []
        for hh in range(4):
            h = g * 4 + hh
            seg = cs[:, h:h + 1] - cs_t[h:h + 1, :]
            decay = jnp.exp(jnp.where(causal, seg, -jnp.inf))
            sc = (cb * decay).astype(BF16)
            ps = slice((h // 2) * LANE, (h // 2 + 1) * LANE)
            yd.append(_dot(sc, x_dt[:, ps]))
        y_diag = jnp.concatenate([jnp.where(lo_half, yd[0], yd[1]), jnp.where(lo_half, yd[2], yd[3])], axis=1)
        y_parts.append(y_diag + y_off)
        ht[g] = h_in * last_e[:, hs] + _dot_tn(bg, x_end[:, hs])
    y = jnp.concatenate(y_parts, axis=1) + dskip_ref[...] * xs
    y_ref[...] = _gated_norm(y, z_ref[...], nw_ref[...])

    @pl.when(c == nc - 1)
    def _():
        for g in range(SSD_GROUPS):
            hout_ref[0, g * 256:(g + 1) * 256, :] = ht[g].T
        cout_ref[0] = xwin[0:8, :][5:8, :]


def _dot_f32ish_left(w01, x):
    hi, lo = _split_hi_lo(x)
    return _dot(w01, hi) + _dot(w01, lo)


def _ssd_prompt_call(z, xbc, dt, w, bsz, seq, chunk):
    nc = seq // chunk
    row = lambda width: pl.BlockSpec((chunk, width), lambda b, c: (b * nc + c, 0))
    return pl.pallas_call(
        functools.partial(_ssd_prompt_body, chunk=chunk),
        grid=(bsz, nc),
        in_specs=[row(SSD_INNER), row(SSD_CH), row(LANE),
                  _full((SSD_CONV, SSD_CH)), _full((1, SSD_CH)), _full((1, LANE)), _full((1, LANE)),
                  _full((1, SSD_INNER)), _full((1, SSD_INNER))],
        out_specs=[row(SSD_INNER),
                   pl.BlockSpec((1, SSD_INNER, SSD_N), lambda b, c: (b, 0, 0)),
                   pl.BlockSpec((1, SSD_CONV - 1, SSD_CH), lambda b, c: (b, 0, 0))],
        out_shape=[jax.ShapeDtypeStruct((bsz * seq, SSD_INNER), F32),
                   jax.ShapeDtypeStruct((bsz, SSD_INNER, SSD_N), F32),
                   jax.ShapeDtypeStruct((bsz, SSD_CONV - 1, SSD_CH), F32)],
        scratch_shapes=[pltpu.VMEM((chunk + 8, SSD_CH), F32), pltpu.VMEM((SSD_GROUPS, SSD_N, 256), F32)],
        compiler_params=_params(("arbitrary", "arbitrary")),
        name="ssd_prompt",
    )(z, xbc, dt, w["conv_w"], w["conv_b"], w["dt_bias"], w["a_log"], w["d_skip_e"], w["ssd_norm"])


def _ssd_decode_body(z_ref, xbc_ref, dt_ref, cbuf_ref, h0_ref, cw_ref, cb_ref, dtb_ref, alog_ref, dskip_ref,
                     nw_ref, y_ref, hout_ref, cout_ref, xt, dect, bsel, csel, yt, xs_keep):
    s = pl.program_id(0)
    n = pl.num_programs(0)

    @pl.when(s == 0)
    def _():
        xnew = xbc_ref[...]
        conv = cb_ref[...] + xnew * cw_ref[3:4, :]
        for i in range(SSD_CONV - 1):
            conv = conv + cbuf_ref[i] * cw_ref[i:i + 1, :]
        cout_ref[0] = cbuf_ref[1]
        cout_ref[1] = cbuf_ref[2]
        cout_ref[2] = xnew
        cv = _silu(conv)
        xs = cv[:, 0:SSD_INNER]
        xs_keep[...] = xs
        bsel[...] = cv[:, SSD_INNER:SSD_INNER + 256]
        csel[...] = cv[:, SSD_INNER + 256:SSD_INNER + 512]
        dt = _softplus(dt_ref[...] + dtb_ref[...])
        da = dt * (-jnp.exp(alog_ref[...]))
        emat = _expand_matrix()
        xt[...] = (xs * _dot_f32ish(dt, emat)).T
        dect[...] = _dot_f32ish(jnp.exp(da), emat).T
        yt[...] = jnp.zeros(yt.shape, F32)

    lane = lax.broadcasted_iota(jnp.int32, xt.shape, 1)
    here = lane == s
    x_col = jnp.sum(jnp.where(here, xt[...], 0.0), axis=1, keepdims=True)
    d_col = jnp.sum(jnp.where(here, dect[...], 0.0), axis=1, keepdims=True)
    b_row = bsel[pl.ds(s, 1), :]
    c_row = csel[pl.ds(s, 1), :]
    ys = []
    for g in range(SSD_GROUPS):
        hs = slice(g * 256, (g + 1) * 256)
        gs = slice(g * SSD_N, (g + 1) * SSD_N)
        h_new = h0_ref[0, hs, :] * d_col[hs] + x_col[hs] * b_row[:, gs]
        hout_ref[0, hs, :] = h_new
        ys.append(jnp.sum(h_new * c_row[:, gs], axis=1, keepdims=True))
    y_col = jnp.concatenate(ys, axis=0)
    yt[...] = jnp.where(here, y_col, yt[...])

    @pl.when(s == n - 1)
    def _():
        xs = xs_keep[...]
        y = yt[...].T + dskip_ref[...] * xs
        y_ref[...] = _gated_norm(y, z_ref[...], nw_ref[...])


def _ssd_decode_call(z, xbc, dt, conv_t, h0, w, layer):
    n = z.shape[0]
    return pl.pallas_call(
        _ssd_decode_body,
        grid=(n,),
        in_specs=[_full((n, SSD_INNER)), _full((n, SSD_CH)), _full((n, LANE)),
                  pl.BlockSpec((SSD_CONV - 1, n, SSD_CH), lambda s: (layer, 0, 0)),
                  pl.BlockSpec((1, SSD_INNER, SSD_N), lambda s: (layer * n + s, 0, 0)),
                  _full((SSD_CONV, SSD_CH)), _full((1, SSD_CH)), _full((1, LANE)), _full((1, LANE)),
                  _full((1, SSD_INNER)), _full((1, SSD_INNER))],
        out_specs=[_full((n, SSD_INNER)),
                   pl.BlockSpec((1, SSD_INNER, SSD_N), lambda s: (s, 0, 0)),
                   _full((SSD_CONV - 1, n, SSD_CH))],
        out_shape=[jax.ShapeDtypeStruct((n, SSD_INNER), F32),
                   jax.ShapeDtypeStruct((n, SSD_INNER, SSD_N), F32),
                   jax.ShapeDtypeStruct((SSD_CONV - 1, n, SSD_CH), F32)],
        scratch_shapes=[pltpu.VMEM((SSD_INNER, n), F32), pltpu.VMEM((SSD_INNER, n), F32),
                        pltpu.VMEM((n, 256), F32), pltpu.VMEM((n, 256), F32),
                        pltpu.VMEM((SSD_INNER, n), F32), pltpu.VMEM((n, SSD_INNER), F32)],
        compiler_params=_params(("arbitrary",)),
        name="ssd_decode",
    )(z, xbc, dt, conv_t, h0, w["conv_w"], w["conv_b"], w["dt_bias"], w["a_log"], w["d_skip_e"], w["ssd_norm"])


FF_CHUNK = 1024


def _out_body(x_ref, sb_ref, ssd_ref, mla_ref, sbg_ref, mlag_ref, wo_ref, g1_ref, b1_ref,
              wup_ref, wdn_ref, g2_ref, b2_ref, o_ref):
    sbn = _rms(sb_ref[...], sbg_ref[...]).astype(BF16)
    mlan = _rms(mla_ref[...], mlag_ref[...]).astype(BF16)
    y = (_dot(sbn, wo_ref[0:SB_W, :])
         + _dot(ssd_ref[...].astype(BF16), wo_ref[SB_W:SB_W + SSD_INNER, :])
         + _dot(mlan, wo_ref[SB_W + SSD_INNER:, :]))
    h = _layernorm(ALPHA * x_ref[...] + y, g1_ref[...], b1_ref[...])
    hb = h.astype(BF16)
    f = jnp.zeros(h.shape, F32)
    for j in range(D_FF // FF_CHUNK):
        fs = slice(j * FF_CHUNK, (j + 1) * FF_CHUNK)
        u = jnp.maximum(_dot(hb, wup_ref[:, fs]), 0.0)
        f = f + _dot((u * u).astype(BF16), wdn_ref[fs, :])
    o_ref[...] = _layernorm(ALPHA * h + f, g2_ref[...], b2_ref[...])


def _out_call(x, sb, ssd, mla, w, tm):
    n = x.shape[0]
    row = lambda width: pl.BlockSpec((tm, width), lambda i: (i, 0))
    const = lambda shape: pl.BlockSpec(shape, lambda i: (0,) * len(shape), pipeline_mode=pl.Buffered(1))
    return pl.pallas_call(
        _out_body,
        grid=(n // tm,),
        in_specs=[row(D_MODEL), row(SB_W), row(SSD_INNER), row(MLA_W),
                  const((1, SB_W)), const((1, MLA_W)), const((D_MODEL, D_MODEL)),
                  const((1, D_MODEL)), const((1, D_MODEL)),
                  const((D_MODEL, D_FF)), const((D_FF, D_MODEL)),
                  const((1, D_MODEL)), const((1, D_MODEL))],
        out_specs=row(D_MODEL),
        out_shape=jax.ShapeDtypeStruct((n, D_MODEL), F32),
        compiler_params=_params(("arbitrary",)),
        name="out_ffn",
    )(x, sb, ssd, mla, w["sb_norm"], w["mla_norm"], w["w_out"], w["ln1_g"], w["ln1_b"],
      w["w_up"], w["w_down"], w["ln2_g"], w["ln2_b"])


def _rot_cols(wm):
    half = wm.shape[-1] // 2
    return jnp.concatenate([-wm[..., half:], wm[..., :half]], axis=-1)


def _layer_weights(l, w_in, sb_norm, conv_w, conv_b, dt_bias, a_log, d_skip, ssd_norm, q_norm, w_uq, kv_norm,
                   w_uk, w_uv, mla_norm, w_out, ln1_g, ln1_b, w_up, w_down, ln2_g, ln2_b):
    wi = w_in[l]
    o = [0, 256, 384, 512, 1024, 2048, 2056, 2312, 2568, 2600]
    sq, sk, sv, z, xbc, dtc, cq, ckv, kpe = [wi[:, o[i]:o[i + 1]] for i in range(9)]
    d = wi.shape[0]
    dt_pad = jnp.concatenate([dtc, jnp.zeros((d, LANE - SSD_HEADS), F32)], axis=1)
    kpe_grp = jnp.concatenate([kpe, _rot_cols(kpe), jnp.zeros((d, LANE - 2 * MLA_ROPE), F32)], axis=1)
    w1 = jnp.concatenate([sq, sk, sv, z, xbc, cq, ckv, dt_pad, kpe_grp], axis=1).astype(BF16)

    uq = w_uq[l]
    zq = lambda width: jnp.zeros((MLA_QR, MLA_HEADS, width), F32)
    pe = uq[..., MLA_NOPE:]
    w2a = jnp.concatenate([uq[..., :MLA_NOPE], pe, zq(HEAD_PAD - MLA_NOPE - MLA_ROPE)], axis=-1)
    w2b = jnp.concatenate([zq(MLA_NOPE), _rot_cols(pe), zq(HEAD_PAD - MLA_NOPE - MLA_ROPE)], axis=-1)
    w2 = jnp.concatenate([w2a.reshape(MLA_QR, -1), w2b.reshape(MLA_QR, -1)], axis=1).astype(BF16)

    uk = w_uk[l]
    wk_prompt = jnp.concatenate([uk, jnp.zeros((MLA_KVR, MLA_HEADS, HEAD_PAD - MLA_NOPE), F32)], axis=-1)
    wk_prompt = wk_prompt.reshape(MLA_KVR, MLA_HEADS * HEAD_PAD).astype(BF16)
    ukt = jnp.transpose(uk, (1, 2, 0))
    wk_decode = jnp.concatenate([ukt, jnp.zeros((MLA_HEADS, HEAD_PAD - MLA_NOPE, MLA_KVR), F32)], axis=1)
    wk_decode = wk_decode.reshape(MLA_HEADS * HEAD_PAD, MLA_KVR).astype(BF16)
    wv = w_uv[l].reshape(MLA_KVR, MLA_W).astype(BF16)

    pad_heads = lambda v: jnp.concatenate([v, jnp.zeros((LANE - SSD_HEADS,), F32)])[None, :]
    return {
        "w1": w1, "w2": w2, "wk_prompt": wk_prompt, "wk_decode": wk_decode, "wv": wv, "wv_t": wv.T,
        "q_norm": q_norm[l][None, :], "kv_norm": kv_norm[l][None, :],
        "conv_w": conv_w[l], "conv_b": conv_b[l][None, :],
        "dt_bias": pad_heads(dt_bias[l]), "a_log": pad_heads(a_log[l]),
        "d_skip_e": jnp.repeat(d_skip[l], SSD_P)[None, :], "ssd_norm": ssd_norm[l][None, :],
        "sb_norm": sb_norm[l][None, :], "mla_norm": mla_norm[l][None, :],
        "w_out": w_out[l].astype(BF16), "ln1_g": ln1_g[l][None, :], "ln1_b": ln1_b[l][None, :],
        "w_up": w_up[l].astype(BF16), "w_down": w_down[l].astype(BF16),
        "ln2_g": ln2_g[l][None, :], "ln2_b": ln2_b[l][None, :],
    }


def _rope_table(pos):
    half = MLA_ROPE // 2
    inv = ROPE_THETA ** (-jnp.arange(half, dtype=F32) / half)
    ang = pos.astype(F32)[:, None] * inv[None, :]
    cos, sin = jnp.cos(ang), jnp.sin(ang)
    pad = jnp.zeros((pos.shape[0], LANE - 2 * MLA_ROPE), F32)
    return jnp.concatenate([cos, cos, sin, sin, pad], axis=1)


def _pick(n, candidates):
    for c in candidates:
        if n % c == 0:
            return c
    return n


def kernel(x_prompt, x_sample, cache_sb_k, cache_sb_v, cache_mla_ckv, cache_mla_kpe, state_ssm, state_conv,
           page_table, w_in, sb_norm, conv_w, conv_b, dt_bias, a_log, d_skip, ssd_norm, q_norm, w_uq, kv_norm,
           w_uk, w_uv, mla_norm, w_out, ln1_g, ln1_b, w_up, w_down, ln2_g, ln2_b):
    bp, seq, d = x_prompt.shape
    nd, dec_seq, _ = x_sample.shape
    assert dec_seq == 1 and d == D_MODEL
    depth = w_in.shape[0]
    n_pages = page_table.shape[1]
    past_len = n_pages * PAGE
    n_pool = cache_sb_k.shape[1]

    tab_p = _rope_table(jnp.arange(seq, dtype=jnp.int32))
    tab_s = _rope_table(jnp.full((nd,), past_len, dtype=jnp.int32))
    kc = jnp.transpose(cache_sb_k, (0, 1, 3, 4, 2)).reshape(depth, n_pool, SB_KV * SB_DIM, PAGE)
    vc = jnp.transpose(cache_sb_v, (0, 1, 3, 4, 2)).reshape(depth, n_pool, SB_KV * SB_DIM, PAGE)
    pc = jnp.transpose(cache_mla_kpe, (0, 1, 3, 2))
    conv_all = jnp.transpose(state_conv, (0, 2, 1, 3)).reshape(depth * (SSD_CONV - 1), nd, SSD_CH)
    h0_all = state_ssm.reshape(depth * nd, SSD_INNER, SSD_N)

    tm_p = _pick(seq, (512, 256, 128))
    tq = _pick(seq, (256, 128))
    chunk = _pick(seq, (128,))
    tk_mla = _pick(tm_p, (256, 128))
    tq_mla = _pick(seq, (2 * tk_mla, tk_mla))

    hp = x_prompt.reshape(bp * seq, d)
    hs = x_sample.reshape(nd, d)
    new_p, new_s = [], []
    for l in range(depth):
        w = _layer_weights(l, w_in, sb_norm, conv_w, conv_b, dt_bias, a_log, d_skip, ssd_norm, q_norm, w_uq,
                           kv_norm, w_uk, w_uv, mla_norm, w_out, ln1_g, ln1_b, w_up, w_down, ln2_g, ln2_b)
        wp = dict(w, wk=w["wk_prompt"], wv=w["wv_t"])
        (sq, sk, sv, z, xbc, dt, ckv, kpe, kb, vb, qh, kh, vmt) = _proj_call(hp, tab_p, wp, False, tm_p, tk_mla)
        sb_o = _sb_prompt_call(sq, kb, vb, bp, seq, tq)
        mla_o = _mla_prompt_call(qh, kh, vmt, bp, seq, tq_mla, tk_mla)
        ssd_o, ssm_new, conv_new = _ssd_prompt_call(z, xbc, dt, w, bp, seq, chunk)
        hp = _out_call(hp, sb_o, ssd_o, mla_o, w, tm_p)
        new_p.append((sk.reshape(bp, seq, SB_KV, SB_DIM), sv.reshape(bp, seq, SB_KV, SB_DIM),
                      ckv.reshape(bp, seq, MLA_KVR), kpe.reshape(bp, seq, MLA_ROPE),
                      ssm_new.reshape(bp, SSD_HEADS, SSD_P, SSD_N), conv_new))
        wd = dict(w, wk=w["wk_decode"])
        (sq, sk, sv, z, xbc, dt, ckv, kpe, qh, qlat) = _proj_call(hs, tab_s, wd, True, nd)
        q4 = sq.reshape(nd, SB_HEADS, SB_DIM)
        zq = jnp.zeros_like(q4)
        grp = (jnp.arange(SB_HEADS) // (SB_HEADS // SB_KV))[None, :, None]
        qbd = jnp.concatenate([jnp.where(grp == 0, q4, zq), jnp.where(grp == 1, q4, zq)], axis=-1)
        qbd = jnp.concatenate([qbd, jnp.zeros_like(qbd)], axis=1)
        sb_o = _sb_decode_call(page_table, qbd, kc, vc, l).reshape(nd, SB_W)
        pad8 = lambda a: jnp.concatenate([a, jnp.zeros_like(a)], axis=1)
        qlat8 = pad8(qlat.reshape(nd, MLA_HEADS, MLA_KVR))
        qpe8 = pad8(qh.reshape(nd, MLA_HEADS, HEAD_PAD)[:, :, MLA_NOPE:MLA_NOPE + MLA_ROPE])
        mla_o = _mla_decode_call(page_table, qlat8, qpe8, ckv[:, None, :], kpe[:, None, :], w["wv"],
                                 cache_mla_ckv, pc, l).reshape(nd, MLA_W)
        ssd_o, ssm_new, conv_new_t = _ssd_decode_call(z, xbc, dt, conv_all, h0_all, w, l)
        hs = _out_call(hs, sb_o, ssd_o, mla_o, w, nd)
        new_s.append((sk.reshape(nd, 1, SB_KV, SB_DIM), sv.reshape(nd, 1, SB_KV, SB_DIM),
                      ckv.reshape(nd, 1, MLA_KVR), kpe.reshape(nd, 1, MLA_ROPE),
                      ssm_new.reshape(nd, SSD_HEADS, SSD_P, SSD_N), jnp.transpose(conv_new_t, (1, 0, 2))))

    stack = lambda states, i: jnp.stack([st[i] for st in states])
    return (hp.reshape(bp, seq, d), hs.reshape(nd, 1, d),
            *[stack(new_p, i) for i in range(6)], *[stack(new_s, i) for i in range(6)])
```
